```python
import math
import jax, jax.numpy as jnp
from jax import lax
import numpy as np

D_MODEL = 1024
BATCH = 2
SEQ = 8192
DEPTH = 4
DEC_BATCH = 32
DEC_SEQ = 1
PAST_LEN = 8192
PAGE_SIZE = 128

N_MIXERS = 2
DIFF_HEAD_DIM = 64
DIFF_V_DIM = 2 * DIFF_HEAD_DIM
DIFF_HEADS = D_MODEL // DIFF_V_DIM
SB_HEAD_DIM = 64
SB_HEADS = D_MODEL // SB_HEAD_DIM
D_FF = 2816
CONV_W = 3
Q_BLOCK = 128
RMS_EPS = 1e-6
NEG_INF = -1e30
F32 = jnp.float32

kernel_name = "diffattn_stickbreak_convffn_hybrid_step"


def rms_norm(x, g):
    xf = x.astype(F32)
    y = xf * lax.rsqrt(jnp.mean(xf * xf, axis=-1, keepdims=True) + RMS_EPS)
    return (y * g.astype(F32)).astype(x.dtype)


def adaln(c, w, b):
    mod = jax.nn.silu(c) @ w + b
    return [m[:, None, :] for m in jnp.split(mod, 6, axis=-1)]


def modulate(h, shift, scale):
    return h * (1 + scale) + shift


def alibi_slopes(n):
    return jnp.asarray(2.0 ** (-8.0 * (np.arange(n) + 1) / n), dtype=F32)


def diff_lambda(lq1, lk1, lq2, lk2, lam_init):
    return (jnp.exp(jnp.sum(lq1.astype(F32) * lk1.astype(F32)))
            - jnp.exp(jnp.sum(lq2.astype(F32) * lk2.astype(F32))) + lam_init)


def sweep_query_blocks(fn, q):
    B, T = q.shape[:2]
    nb = T // Q_BLOCK
    qb = jnp.swapaxes(q.reshape((B, nb, Q_BLOCK) + q.shape[2:]), 0, 1)
    pos = jnp.arange(T, dtype=jnp.int32).reshape(nb, Q_BLOCK)
    out = lax.map(lambda a: fn(a[0], a[1]), (qb, pos))
    out = jnp.swapaxes(out, 0, 1)
    return out.reshape((B, T) + out.shape[3:])


def gather_pages(cache, layer, page_table):
    g = cache[layer, page_table]
    return g.reshape((g.shape[0], -1) + g.shape[3:])


def diff_attn_core(q, k, v, qpos, kpos, lam, slopes):
    B, Tk = k.shape[:2]
    k = k.reshape(B, Tk, DIFF_HEADS, 2, DIFF_HEAD_DIM)
    s = jnp.einsum("bqhcd,bkhcd->bhcqk", q, k, preferred_element_type=F32) * (DIFF_HEAD_DIM ** -0.5)
    dist = (qpos[:, None] - kpos[None, :]).astype(F32)
    s = s - slopes[:, None, None, None] * dist
    s = jnp.where(dist >= 0, s, NEG_INF)
    p = jax.nn.softmax(s, axis=-1)
    a = p[:, :, 0] - lam * p[:, :, 1]
    return jnp.einsum("bhqk,bkhe->bqhe", a.astype(v.dtype), v)


def sb_core(q, k, v, qpos, kpos):
    z = jnp.einsum("bqhd,bkhd->bhqk", q, k, preferred_element_type=F32) * (SB_HEAD_DIM ** -0.5)
    valid = kpos[None, :] < qpos[:, None]
    log_beta = jax.nn.log_sigmoid(z)
    log_1m = jnp.where(valid, jax.nn.log_sigmoid(-z), 0.0)
    after = lax.cumsum(log_1m, axis=3, reverse=True) - log_1m
    a = jnp.where(valid, jnp.exp(log_beta + after), 0.0)
    return jnp.einsum("bhqk,bkhd->bqhd", a.astype(v.dtype), v)


def diff_mixer(h, past_k, past_v, w_qkv, lam, lam_init, g_sub, w_o, slopes):
    B, T, _ = h.shape
    q, k, v = jnp.split(h @ w_qkv, 3, axis=-1)
    q = q.reshape(B, T, DIFF_HEADS, 2, DIFF_HEAD_DIM)
    k = k.reshape(B, T, DIFF_HEADS, DIFF_V_DIM)
    v = v.reshape(B, T, DIFF_HEADS, DIFF_V_DIM)
    if past_k is None:
        kpos = jnp.arange(T, dtype=jnp.int32)
        o = sweep_query_blocks(lambda qb, qp: diff_attn_core(qb, k, v, qp, kpos, lam, slopes), q)
    else:
        past = past_k.shape[1]
        k_all = jnp.concatenate([past_k.astype(k.dtype), k], axis=1)
        v_all = jnp.concatenate([past_v.astype(v.dtype), v], axis=1)
        o = diff_attn_core(q, k_all, v_all, past + jnp.arange(T, dtype=jnp.int32),
                           jnp.arange(past + T, dtype=jnp.int32), lam, slopes)
    o = rms_norm(o, g_sub) * (1.0 - lam_init)
    return o.reshape(B, T, D_MODEL) @ w_o, (k, v)


def sb_mixer(h, past_k, past_v, w_qkv, w_o):
    B, T, _ = h.shape
    q, k, v = [a.reshape(B, T, SB_HEADS, SB_HEAD_DIM) for a in jnp.split(h @ w_qkv, 3, axis=-1)]
    if past_k is None:
        kpos = jnp.arange(T, dtype=jnp.int32)
        o = sweep_query_blocks(lambda qb, qp: sb_core(qb, k, v, qp, kpos), q)
    else:
        past = past_k.shape[1]
        k_all = jnp.concatenate([past_k.astype(k.dtype), k], axis=1)
        v_all = jnp.concatenate([past_v.astype(v.dtype), v], axis=1)
        o = sb_core(q, k_all, v_all, past + jnp.arange(T, dtype=jnp.int32),
                    jnp.arange(past + T, dtype=jnp.int32))
    return o.reshape(B, T, D_MODEL) @ w_o, (k, v)


def conv_ffn(h, hist, w_up, conv_w, conv_b, w_down):
    T = h.shape[1]
    u = h @ w_up
    ext = jnp.concatenate([hist.astype(u.dtype), u], axis=1)
    y = conv_b
    for i in range(CONV_W):
        y = y + ext[:, i:i + T] * conv_w[i]
    val, gate = jnp.split(y, 2, axis=-1)
    out = (jax.nn.gelu(gate, approximate=True) * val) @ w_down
    return out, ext[:, T:]


def trunk_layer(x, c, layer, mixer, hist, w_ada, b_ada, g_mix_pre, g_mix_post, g_ffn_pre, g_ffn_post,
                w_up, conv_w, conv_b, w_down):
    sh_m, sc_m, gt_m, sh_f, sc_f, gt_f = adaln(c, w_ada[layer], b_ada[layer])
    y, kv = mixer(modulate(rms_norm(x, g_mix_pre[layer]), sh_m, sc_m))
    x = x + gt_m * rms_norm(y, g_mix_post[layer])
    f, new_hist = conv_ffn(modulate(rms_norm(x, g_ffn_pre[layer]), sh_f, sc_f), hist,
                           w_up[layer], conv_w[layer], conv_b[layer], w_down[layer])
    x = x + gt_f * rms_norm(f, g_ffn_post[layer])
    return x, kv, new_hist


def setup_inputs(seed: int = 0) -> dict:
    key = jax.random.key(seed)
    ks = list(jax.random.split(key, 32))

    def nrm(shape, scale=1.0):
        return scale * jax.random.normal(ks.pop(), shape, F32)

    D = D_MODEL
    F2 = 2 * D_FF
    n_diff = len(range(0, DEPTH, N_MIXERS))
    n_sb = len(range(1, DEPTH, N_MIXERS))
    n_pages = PAST_LEN // PAGE_SIZE
    n_used = DEC_BATCH * n_pages
    n_pool = n_used + n_used // 4
    page_table = jax.random.permutation(ks.pop(), n_pool)[:n_used].reshape(DEC_BATCH, n_pages).astype(jnp.int32)
    return {
        "x_prompt": nrm((BATCH, SEQ, D)),
        "x_sample": nrm((DEC_BATCH, DEC_SEQ, D)),
        "cache_k_diff": nrm((n_diff, n_pool, PAGE_SIZE, DIFF_HEADS, DIFF_V_DIM)),
        "cache_v_diff": nrm((n_diff, n_pool, PAGE_SIZE, DIFF_HEADS, DIFF_V_DIM)),
        "cache_k_sb": nrm((n_sb, n_pool, PAGE_SIZE, SB_HEADS, SB_HEAD_DIM)),
        "cache_v_sb": nrm((n_sb, n_pool, PAGE_SIZE, SB_HEADS, SB_HEAD_DIM)),
        "state_conv": nrm((DEPTH, DEC_BATCH, CONV_W - 1, F2)),
        "page_table": page_table,
        "c_prompt": nrm((BATCH, D)),
        "c_sample": nrm((DEC_BATCH, D)),
        "w_ada": nrm((DEPTH, D, 6 * D), 0.5 * D ** -0.5),
        "b_ada": nrm((DEPTH, 6 * D), 0.02),
        "g_mix_pre": 1.0 + nrm((DEPTH, D), 0.1),
        "g_mix_post": 1.0 + nrm((DEPTH, D), 0.1),
        "g_ffn_pre": 1.0 + nrm((DEPTH, D), 0.1),
        "g_ffn_post": 1.0 + nrm((DEPTH, D), 0.1),
        "w_qkv_diff": nrm((n_diff, D, 3 * D), D ** -0.5),
        "lambda_q1": nrm((n_diff, DIFF_HEAD_DIM), 0.1),
        "lambda_k1": nrm((n_diff, DIFF_HEAD_DIM), 0.1),
        "lambda_q2": nrm((n_diff, DIFF_HEAD_DIM), 0.1),
        "lambda_k2": nrm((n_diff, DIFF_HEAD_DIM), 0.1),
        "g_subln": 1.0 + nrm((n_diff, DIFF_V_DIM), 0.1),
        "w_o_diff": nrm((n_diff, D, D), D ** -0.5),
        "w_qkv_sb": nrm((n_sb, D, 3 * D), D ** -0.5),
        "w_o_sb": nrm((n_sb, D, D), D ** -0.5),
        "w_up": nrm((DEPTH, D, F2), D ** -0.5),
        "conv_w": nrm((DEPTH, CONV_W, F2), CONV_W ** -0.5),
        "conv_b": nrm((DEPTH, F2), 0.02),
        "w_down": nrm((DEPTH, D_FF, D), D_FF ** -0.5),
    }


def reference(x_prompt, x_sample, cache_k_diff, cache_v_diff, cache_k_sb, cache_v_sb, state_conv, page_table,
              c_prompt, c_sample, w_ada, b_ada, g_mix_pre, g_mix_post, g_ffn_pre, g_ffn_post,
              w_qkv_diff, lambda_q1, lambda_k1, lambda_q2, lambda_k2, g_subln, w_o_diff,
              w_qkv_sb, w_o_sb, w_up, conv_w, conv_b, w_down):
    slopes = alibi_slopes(DIFF_HEADS)
    xp, xs = x_prompt, x_sample
    hist_p0 = jnp.zeros((xp.shape[0], CONV_W - 1, w_up.shape[-1]), xp.dtype)
    kd_p, vd_p, ks_p, vs_p, kd_s, vd_s, ks_s, vs_s, conv_p, conv_s = ([] for _ in range(10))
    shared = (w_ada, b_ada, g_mix_pre, g_mix_post, g_ffn_pre, g_ffn_post, w_up, conv_w, conv_b, w_down)
    for layer in range(DEPTH):
        j = layer // N_MIXERS
        if layer % N_MIXERS == 0:
            lam_init = 0.8 - 0.6 * math.exp(-0.3 * layer)
            lam = diff_lambda(lambda_q1[j], lambda_k1[j], lambda_q2[j], lambda_k2[j], lam_init)
            past_k = gather_pages(cache_k_diff, j, page_table)
            past_v = gather_pages(cache_v_diff, j, page_table)
            mix_p = lambda h: diff_mixer(h, None, None, w_qkv_diff[j], lam, lam_init, g_subln[j], w_o_diff[j], slopes)
            mix_s = lambda h: diff_mixer(h, past_k, past_v, w_qkv_diff[j], lam, lam_init, g_subln[j], w_o_diff[j], slopes)
            store = (kd_p, vd_p, kd_s, vd_s)
        else:
            past_k = gather_pages(cache_k_sb, j, page_table)
            past_v = gather_pages(cache_v_sb, j, page_table)
            mix_p = lambda h: sb_mixer(h, None, None, w_qkv_sb[j], w_o_sb[j])
            mix_s = lambda h: sb_mixer(h, past_k, past_v, w_qkv_sb[j], w_o_sb[j])
            store = (ks_p, vs_p, ks_s, vs_s)
        xp, (kp, vp), hp = trunk_layer(xp, c_prompt, layer, mix_p, hist_p0, *shared)
        xs, (kn, vn), hs = trunk_layer(xs, c_sample, layer, mix_s, state_conv[layer], *shared)
        store[0].append(kp)
        store[1].append(vp)
        store[2].append(kn)
        store[3].append(vn)
        conv_p.append(hp)
        conv_s.append(hs)
    return (xp, xs,
            jnp.stack(kd_p), jnp.stack(vd_p), jnp.stack(ks_p), jnp.stack(vs_p), jnp.stack(conv_p),
            jnp.stack(kd_s), jnp.stack(vd_s), jnp.stack(ks_s), jnp.stack(vs_s), jnp.stack(conv_s))
```

```python
import functools
import math

import numpy as np
import jax
import jax.numpy as jnp
from jax import lax
from jax.experimental import pallas as pl
from jax.experimental.pallas import tpu as pltpu

F32 = jnp.float32
BF16 = jnp.bfloat16
RMS_EPS = 1e-6
NEG_INF = -1e30
LANES = 128
SUBLANES = 8
HEAD_LANES = 64
CONV_W = 3
SB_EXIT = -104.0
VMEM_LIMIT = 56 * 1024 * 1024
_NT = (((1,), (1,)), ((), ()))

ROW_TILE = 512
ATTN_TILE_DIFF = 512
ATTN_TILE_SB = 256
FF_CHUNK = 256
PAGES_PER_STEP = 4


def _rms(x, g):
    return x * lax.rsqrt(jnp.mean(x * x, axis=-1, keepdims=True) + RMS_EPS) * g


def _params(*sem):
    return pltpu.CompilerParams(dimension_semantics=sem, vmem_limit_bytes=VMEM_LIMIT)


def _mod_kernel(c_ref, w_ref, b_ref, o_ref):
    c = c_ref[...]
    s = c * jax.nn.sigmoid(c)
    o_ref[0] = jnp.dot(s.astype(BF16), w_ref[0].astype(BF16), preferred_element_type=F32) + b_ref[0]


def _modulation(c_all, w_ada, b_ada):
    depth, d, n = w_ada.shape
    r = c_all.shape[0]
    tn = n // 4
    return pl.pallas_call(
        _mod_kernel,
        out_shape=jax.ShapeDtypeStruct((depth, r, n), F32),
        grid=(depth, n // tn),
        in_specs=[pl.BlockSpec((r, d), lambda l, j: (0, 0)),
                  pl.BlockSpec((1, d, tn), lambda l, j: (l, 0, j)),
                  pl.BlockSpec((1, 1, tn), lambda l, j: (l, 0, j))],
        out_specs=pl.BlockSpec((1, r, tn), lambda l, j: (l, 0, j)),
        compiler_params=_params("arbitrary", "arbitrary"),
        name="adaln_modulation",
    )(c_all, w_ada, b_ada.reshape(depth, 1, n))


def _mod_spec(rm, tm, d, chunk):
    if rm == 1:
        return pl.BlockSpec((1, 1, d), lambda b, t: (b, 0, chunk))
    return pl.BlockSpec((1, tm, d), lambda b, t: (b, t, chunk))


def _prenorm(x_ref, g_ref, sh_ref, sc_ref):
    return (_rms(x_ref[0], g_ref[...]) * (1.0 + sc_ref[0]) + sh_ref[0]).astype(BF16)


def _qkv_diff_kernel(x_ref, g_ref, sh_ref, sc_ref, w_ref, *rest, q_scale, aliased):
    if aliased:
        rest = rest[2:]
    q_ref, kb_ref, vb_ref, kf_ref, vf_ref = rest
    tm, d = x_ref.shape[1], x_ref.shape[2]
    nh = d // LANES
    h = _prenorm(x_ref, g_ref, sh_ref, sc_ref)
    q = jnp.dot(h, w_ref[:, 0:d], preferred_element_type=F32)
    q_ref[0] = (q * q_scale).astype(BF16)
    for src, bf_ref, f_ref in ((1, kb_ref, kf_ref), (2, vb_ref, vf_ref)):
        y = jnp.dot(h, w_ref[:, src * d:(src + 1) * d], preferred_element_type=F32)
        bf_ref[0] = y.astype(BF16)
        for hd in range(nh):
            f_ref[0, 0, pl.ds(hd, tm, stride=nh), :] = y[:, hd * LANES:(hd + 1) * LANES]


def _qkv_diff(x, mod, g_pre, w_bf, k_stack, v_stack, j, n_stack, tm):
    nb, tn, d = x.shape
    nh = d // LANES
    row = pl.BlockSpec((1, tm, d), lambda b, t: (b, t, 0))
    stack_spec = pl.BlockSpec((1, 1, tm * nh, LANES), lambda b, t: (j, b, t, 0))
    stack_shape = jax.ShapeDtypeStruct((n_stack, nb, tn * nh, LANES), F32)
    bfo = jax.ShapeDtypeStruct((nb, tn, d), BF16)
    aliased = k_stack is not None
    in_specs = [row, pl.BlockSpec((1, d), lambda b, t: (0, 0)),
                _mod_spec(1, tm, d, 0), _mod_spec(1, tm, d, 1),
                pl.BlockSpec((d, 3 * d), lambda b, t: (0, 0))]
    args = [x, g_pre, mod, mod, w_bf]
    aliases = {}
    if aliased:
        in_specs += [pl.BlockSpec(memory_space=pl.ANY)] * 2
        args += [k_stack, v_stack]
        aliases = {5: 3, 6: 4}
    return pl.pallas_call(
        functools.partial(_qkv_diff_kernel, q_scale=HEAD_LANES ** -0.5, aliased=aliased),
        out_shape=(bfo, bfo, bfo, stack_shape, stack_shape),
        grid=(nb, tn // tm),
        in_specs=in_specs,
        out_specs=(row, row, row, stack_spec, stack_spec),
        input_output_aliases=aliases,
        compiler_params=_params("arbitrary", "arbitrary"),
        name="prenorm_qkv_diff",
    )(*args)


def _qkv_sb_kernel(x_ref, g_ref, sh_ref, sc_ref, wq_ref, wkt_ref, wvt_ref, *rest, q_scale, aliased):
    if aliased:
        rest = rest[2:]
    q_ref, kb_ref, vb_ref, kf_ref, vf_ref = rest
    h = _prenorm(x_ref, g_ref, sh_ref, sc_ref)
    q_ref[0] = (jnp.dot(h, wq_ref[...], preferred_element_type=F32) * q_scale).astype(BF16)
    for wt_ref, bf_ref, f_ref in ((wkt_ref, kb_ref, kf_ref), (wvt_ref, vb_ref, vf_ref)):
        yt = lax.dot_general(wt_ref[...], h, _NT, preferred_element_type=F32)
        bf_ref[0] = yt.astype(BF16)
        f_ref[0, 0] = yt


def _qkv_sb(x, mod, g_pre, wq_bf, wkt_bf, wvt_bf, k_stack, v_stack, j, n_stack, tm):
    nb, tn, d = x.shape
    row = pl.BlockSpec((1, tm, d), lambda b, t: (b, t, 0))
    col = pl.BlockSpec((1, d, tm), lambda b, t: (b, 0, t))
    stack_spec = pl.BlockSpec((1, 1, d, tm), lambda b, t: (j, b, 0, t))
    stack_shape = jax.ShapeDtypeStruct((n_stack, nb, d, tn), F32)
    wspec = pl.BlockSpec((d, d), lambda b, t: (0, 0))
    aliased = k_stack is not None
    in_specs = [row, pl.BlockSpec((1, d), lambda b, t: (0, 0)),
                _mod_spec(1, tm, d, 0), _mod_spec(1, tm, d, 1), wspec, wspec, wspec]
    args = [x, g_pre, mod, mod, wq_bf, wkt_bf, wvt_bf]
    aliases = {}
    if aliased:
        in_specs += [pl.BlockSpec(memory_space=pl.ANY)] * 2
        args += [k_stack, v_stack]
        aliases = {7: 3, 8: 4}
    return pl.pallas_call(
        functools.partial(_qkv_sb_kernel, q_scale=HEAD_LANES ** -0.5, aliased=aliased),
        out_shape=(jax.ShapeDtypeStruct((nb, tn, d), BF16), jax.ShapeDtypeStruct((nb, d, tn), BF16),
                   jax.ShapeDtypeStruct((nb, d, tn), BF16), stack_shape, stack_shape),
        grid=(nb, tn // tm),
        in_specs=in_specs,
        out_specs=(row, col, col, stack_spec, stack_spec),
        input_output_aliases=aliases,
        compiler_params=_params("arbitrary", "arbitrary"),
        name="prenorm_qkv_sb",
    )(*args)


def _qkv_sample_kernel(x_ref, g_ref, sh_ref, sc_ref, w_ref, q_ref, k_ref, v_ref, *, q_scale):
    d = x_ref.shape[-1]
    h = _prenorm(x_ref, g_ref, sh_ref, sc_ref)
    q_ref[...] = jnp.dot(h, w_ref[:, 0:d], preferred_element_type=F32) * q_scale
    k_ref[...] = jnp.dot(h, w_ref[:, d:2 * d], preferred_element_type=F32)
    v_ref[...] = jnp.dot(h, w_ref[:, 2 * d:3 * d], preferred_element_type=F32)


def _qkv_sample(x, mod, g_pre, w_bf):
    _, r, d = x.shape
    o = jax.ShapeDtypeStruct((r, d), F32)
    ospec = pl.BlockSpec((r, d), lambda b, t: (0, 0))
    return pl.pallas_call(
        functools.partial(_qkv_sample_kernel, q_scale=HEAD_LANES ** -0.5),
        out_shape=(o, o, o),
        grid=(1, 1),
        in_specs=[pl.BlockSpec((1, r, d), lambda b, t: (0, 0, 0)), pl.BlockSpec((1, d), lambda b, t: (0, 0)),
                  _mod_spec(r, r, d, 0), _mod_spec(r, r, d, 1),
                  pl.BlockSpec((d, 3 * d), lambda b, t: (0, 0))],
        out_specs=(ospec, ospec, ospec),
        compiler_params=_params("arbitrary", "arbitrary"),
        name="prenorm_qkv_sample",
    )(x, g_pre, mod, mod, w_bf)


def _diff_lambda(lam_ref, lam_init):
    lp = lam_ref[...]
    a = jnp.sum(lp[0:1] * lp[1:2], axis=1, keepdims=True)
    b = jnp.sum(lp[2:3] * lp[3:4], axis=1, keepdims=True)
    return jnp.exp(a) - jnp.exp(b) + lam_init


def _diff_attn_kernel(q_ref, k_ref, v_ref, kf_ref, qf_ref, lam_ref, g_ref, o_ref, acc_ref, m_ref, *, t, lam_init):
    qi = pl.program_id(2)
    lane = lax.broadcasted_iota(jnp.int32, (1, LANES), 1)
    low = lane < HEAD_LANES
    q = q_ref[0]
    qf = qf_ref[0]
    qmaps = (jnp.where(low, q, qf), jnp.where(low, qf, q))
    ones_blk = jnp.broadcast_to(jnp.where(lane == 0, 1.0, 0.0).astype(BF16), (t, LANES))
    acc_ref[...] = jnp.zeros_like(acc_ref)
    m_ref[...] = jnp.full_like(m_ref, NEG_INF)

    def block(j, masked):
        off = pl.multiple_of(j * t, t)
        k = k_ref[0, pl.ds(off, t), :]
        kf = kf_ref[pl.ds(off, t), :]
        v2 = jnp.concatenate([v_ref[0, pl.ds(off, t), :], ones_blk], axis=1)
        kmaps = (jnp.where(low, k, kf), jnp.where(low, kf, k))
        for c in range(2):
            s = lax.dot_general(qmaps[c], kmaps[c], _NT, preferred_element_type=F32)
            if masked:
                row = lax.broadcasted_iota(jnp.int32, (t, t), 0)
                col = lax.broadcasted_iota(jnp.int32, (t, t), 1)
                s = jnp.where(col <= row, s, NEG_INF)
            m_old = m_ref[c]
            m_new = jnp.maximum(m_old, jnp.max(s, axis=1, keepdims=True))
            p = jnp.exp(s - m_new)
            acc_ref[c] = jnp.exp(m_old - m_new) * acc_ref[c] + jnp.dot(p.astype(BF16), v2, preferred_element_type=F32)
            m_ref[c] = m_new

    def body(j, carry):
        block(j, False)
        return carry

    lax.fori_loop(0, qi, body, 0)
    block(qi, True)

    lam = _diff_lambda(lam_ref, lam_init)
    a1 = acc_ref[0]
    a2 = acc_ref[1]
    o = a1[:, 0:LANES] / a1[:, LANES:LANES + 1] - lam * (a2[:, 0:LANES] / a2[:, LANES:LANES + 1])
    o_ref[0] = (_rms(o, g_ref[...]) * (1.0 - lam_init)).astype(BF16)


def _alibi_slopes(n_heads):
    return [2.0 ** (-8.0 * (i + 1) / n_heads) for i in range(n_heads)]


def _alibi_tables(t_len, n_heads):
    pos = np.arange(t_len)
    parts = np.stack([(pos >> 8) << 8, ((pos >> 4) & 15) << 4, pos & 15], axis=1).astype(np.float32)
    kf = np.zeros((t_len, LANES), np.float32)
    rem = jnp.asarray(_alibi_slopes(n_heads), F32)
    pieces = []
    for _ in range(3):
        p = rem.astype(BF16).astype(F32)
        pieces.append(p)
        rem = rem - p
    qf = jnp.zeros((n_heads, 1, LANES), F32)
    for base in (0, HEAD_LANES):
        for pi in range(3):
            for f in range(3):
                kf[:, base + 3 * pi + f] = parts[:, f]
                qf = qf.at[:, 0, base + 3 * pi + f].set(pieces[pi])
    return jnp.asarray(kf, BF16), qf.astype(BF16)


def _diff_attention(q, k, v, kfeat, qfeat, lam_p, g_sub, lam_init, t):
    b, t_len, d = q.shape
    nh = d // LANES
    qspec = pl.BlockSpec((1, t, LANES), lambda bi, h, i: (bi, i, h))
    kvspec = pl.BlockSpec((1, t_len, LANES), lambda bi, h, i: (bi, 0, h))
    return pl.pallas_call(
        functools.partial(_diff_attn_kernel, t=t, lam_init=lam_init),
        out_shape=jax.ShapeDtypeStruct((b, t_len, d), BF16),
        grid=(b, nh, t_len // t),
        in_specs=[qspec, kvspec, kvspec,
                  pl.BlockSpec((t_len, LANES), lambda bi, h, i: (0, 0)),
                  pl.BlockSpec((1, 1, LANES), lambda bi, h, i: (h, 0, 0)),
                  pl.BlockSpec(lam_p.shape, lambda bi, h, i: (0, 0)),
                  pl.BlockSpec((1, LANES), lambda bi, h, i: (0, 0))],
        out_specs=qspec,
        scratch_shapes=[pltpu.VMEM((2, t, 2 * LANES), F32), pltpu.VMEM((2, t, 1), F32)],
        compiler_params=_params("arbitrary", "arbitrary", "arbitrary"),
        name="diff_attention_prompt",
    )(q, k, v, kfeat, qfeat, lam_p, g_sub)


def _sb_weights(z, valid):
    sp = jnp.maximum(z, 0.0) + jnp.log(1.0 + jnp.exp(-jnp.abs(z)))
    l1m = -sp
    if valid is not None:
        l1m = jnp.where(valid, l1m, 0.0)
    return l1m, z - sp


def _suffix_sums(l1m, tri):
    hi = l1m.astype(BF16)
    lo = (l1m - hi.astype(F32)).astype(BF16)
    return jnp.dot(hi, tri, preferred_element_type=F32) + jnp.dot(lo, tri, preferred_element_type=F32)


def _sb_attn_kernel(q_ref, kt_ref, vt_ref, tri_ref, o_ref, acc_ref, carry_ref, *, t):
    qi = pl.program_id(2)
    lane = lax.broadcasted_iota(jnp.int32, (1, LANES), 1)
    q = q_ref[0]
    tri = tri_ref[...]
    zero = jnp.zeros_like(q)

    for hh in range(2):
        sel = (lane < HEAD_LANES) if hh == 0 else (lane >= HEAD_LANES)
        qh = jnp.where(sel, q, zero)
        acc_ref[hh] = jnp.zeros((t, LANES), F32)
        carry_ref[...] = jnp.zeros_like(carry_ref)

        def block(j, masked, qh=qh, hh=hh):
            off = pl.multiple_of(j * t, t)
            kt = kt_ref[0, :, pl.ds(off, t)]
            vt = vt_ref[0, :, pl.ds(off, t)]
            z = jnp.dot(qh, kt, preferred_element_type=F32)
            valid = None
            if masked:
                row = lax.broadcasted_iota(jnp.int32, (t, t), 0)
                col = lax.broadcasted_iota(jnp.int32, (t, t), 1)
                valid = col < row
            l1m, lb = _sb_weights(z, valid)
            c = carry_ref[...]
            a = jnp.exp(lb + (_suffix_sums(l1m, tri) + c))
            if masked:
                a = jnp.where(valid, a, 0.0)
            acc_ref[hh] += lax.dot_general(a.astype(BF16), vt, _NT, preferred_element_type=F32)
            c_new = c + jnp.sum(l1m, axis=1, keepdims=True)
            carry_ref[...] = c_new
            return jnp.max(c_new)

        cmax = block(qi, True)

        def cond(st):
            return jnp.logical_and(st[0] >= 0, st[1] > SB_EXIT)

        def body(st, block=block):
            return st[0] - 1, block(st[0], False)

        lax.while_loop(cond, body, (qi - 1, cmax))

    o_ref[0] = jnp.where(lane < HEAD_LANES, acc_ref[0], acc_ref[1]).astype(BF16)


def _suffix_matrix(n):
    j = np.arange(n)[:, None]
    s = np.arange(n)[None, :]
    return jnp.asarray((j > s).astype(np.float32), BF16)


def _sb_attention(q, kt, vt, t):
    b, t_len, d = q.shape
    qspec = pl.BlockSpec((1, t, LANES), lambda bi, h, i: (bi, i, h))
    kvspec = pl.BlockSpec((1, LANES, t_len), lambda bi, h, i: (bi, h, 0))
    return pl.pallas_call(
        functools.partial(_sb_attn_kernel, t=t),
        out_shape=jax.ShapeDtypeStruct((b, t_len, d), BF16),
        grid=(b, d // LANES, t_len // t),
        in_specs=[qspec, kvspec, kvspec, pl.BlockSpec((t, t), lambda bi, h, i: (0, 0))],
        out_specs=qspec,
        scratch_shapes=[pltpu.VMEM((2, t, LANES), F32), pltpu.VMEM((t, 1), F32)],
        compiler_params=_params("arbitrary", "arbitrary", "arbitrary"),
        name="sb_attention_prompt",
    )(q, kt, vt, _suffix_matrix(t))


def _gelu_tanh(x):
    return 0.5 * x * (1.0 + jnp.tanh(math.sqrt(2.0 / math.pi) * (x + 0.044715 * (x * x * x))))


def _post_ffn_kernel(x_ref, o_ref, wo_ref, gmp_ref, gtm_ref, gfp_ref, shf_ref, scf_ref, gtf_ref,
                     wup_ref, cw_ref, cb_ref, wdn_ref, gfo_ref,
                     xo_ref, hist_ref, carry_ref, ext_ref, acc_ref, *, tm, cw, ff):
    ti = pl.program_id(1)
    nt = pl.num_programs(1)

    @pl.when(ti == 0)
    def _():
        carry_ref[...] = jnp.zeros_like(carry_ref)

    y = jnp.dot(o_ref[0], wo_ref[...], preferred_element_type=F32)
    x1 = x_ref[0] + gtm_ref[0] * _rms(y, gmp_ref[...])
    h = (_rms(x1, gfp_ref[...]) * (1.0 + scf_ref[0]) + shf_ref[0]).astype(BF16)
    acc_ref[...] = jnp.zeros_like(acc_ref)

    def conv(off):
        u = jnp.dot(h, wup_ref[:, off:off + cw], preferred_element_type=F32)
        ext_ref[0:SUBLANES, :] = carry_ref[:, off:off + cw]
        ext_ref[SUBLANES:SUBLANES + tm, :] = u
        y = (cb_ref[:, off:off + cw]
             + ext_ref[SUBLANES - 2:SUBLANES - 2 + tm, :] * cw_ref[0:1, off:off + cw]
             + ext_ref[SUBLANES - 1:SUBLANES - 1 + tm, :] * cw_ref[1:2, off:off + cw]
             + u * cw_ref[2:3, off:off + cw])
        carry_ref[:, off:off + cw] = u[tm - SUBLANES:tm, :]

        @pl.when(ti == nt - 1)
        def _():
            hist_ref[0, :, off:off + cw] = u[tm - (CONV_W - 1):tm, :]

        return y

    for c in range(ff // cw):
        val = conv(c * cw)
        gate = conv(ff + c * cw)
        g = (_gelu_tanh(gate) * val).astype(BF16)
        acc_ref[...] += jnp.dot(g, wdn_ref[c * cw:(c + 1) * cw, :], preferred_element_type=F32)

    xo_ref[0] = x1 + gtf_ref[0] * _rms(acc_ref[...], gfo_ref[...])


def _post_ffn(x, o, mod, wo_bf, g_mix_post, g_ffn_pre, wup_bf, conv_w, conv_b, wdn_bf, g_ffn_post, tm, cw):
    nb, tn, d = x.shape
    ff = wdn_bf.shape[0]
    rm = mod.shape[1]
    row = pl.BlockSpec((1, tm, d), lambda b, t: (b, t, 0))
    vec = pl.BlockSpec((1, d), lambda b, t: (0, 0))

    def full(a):
        return pl.BlockSpec(a.shape, lambda b, t: (0,) * a.ndim)

    return pl.pallas_call(
        functools.partial(_post_ffn_kernel, tm=tm, cw=cw, ff=ff),
        out_shape=(jax.ShapeDtypeStruct((nb, tn, d), F32),
                   jax.ShapeDtypeStruct((nb, CONV_W - 1, 2 * ff), F32)),
        grid=(nb, tn // tm),
        in_specs=[row, row, full(wo_bf), vec, _mod_spec(rm, tm, d, 2), vec,
                  _mod_spec(rm, tm, d, 3), _mod_spec(rm, tm, d, 4), _mod_spec(rm, tm, d, 5),
                  full(wup_bf), full(conv_w), full(conv_b), full(wdn_bf), vec],
        out_specs=(row, pl.BlockSpec((1, CONV_W - 1, 2 * ff), lambda b, t: (b, 0, 0))),
        scratch_shapes=[pltpu.VMEM((SUBLANES, 2 * ff), F32), pltpu.VMEM((tm + SUBLANES, cw), F32),
                        pltpu.VMEM((tm, d), F32)],
        compiler_params=_params("arbitrary", "arbitrary"),
        name="attn_out_convffn_prompt",
    )(x, o, wo_bf, g_mix_post, mod, g_ffn_pre, mod, mod, mod, wup_bf, conv_w, conv_b, wdn_bf, g_ffn_post)


def _head_rows(q_row, n_rows):
    d = q_row.shape[-1]
    r = lax.broadcasted_iota(jnp.int32, (n_rows, d), 0)
    grp = lax.broadcasted_iota(jnp.int32, (n_rows, d), 1) // HEAD_LANES
    return jnp.where(grp == r, jnp.broadcast_to(q_row, (n_rows, d)), 0.0)


def _dec_diff_kernel(pt_ref, q_ref, kn_ref, vn_ref, lam_ref, g_ref, *rest, past, page, lam_init, slopes):
    k_refs = rest[0:PAGES_PER_STEP]
    v_refs = rest[PAGES_PER_STEP:2 * PAGES_PER_STEP]
    o_ref, qh_ref, acc_ref, m_ref, l_ref = rest[2 * PAGES_PER_STEP:]
    g = pl.program_id(1)
    ng = pl.num_programs(1)
    nh = len(slopes)
    ntok = PAGES_PER_STEP * page

    @pl.when(g == 0)
    def _():
        for h in range(nh):
            sl = slice(h * LANES, (h + 1) * LANES)
            qh = _head_rows(q_ref[0][:, sl], SUBLANES)
            qh_ref[h] = qh.astype(BF16)
            m_ref[h] = jnp.sum(qh * kn_ref[0][:, sl], axis=1, keepdims=True)
            l_ref[h] = jnp.ones((SUBLANES, 1), F32)
            acc_ref[h] = jnp.broadcast_to(vn_ref[0][:, sl], (SUBLANES, LANES))

    pos = g * ntok + lax.broadcasted_iota(jnp.int32, (1, ntok), 1)
    dist = (past - pos).astype(F32)
    for h in range(nh):
        k = jnp.concatenate([r[0, 0, pl.ds(h, page, stride=nh), :] for r in k_refs], axis=0).astype(BF16)
        v = jnp.concatenate([r[0, 0, pl.ds(h, page, stride=nh), :] for r in v_refs], axis=0).astype(BF16)
        s = lax.dot_general(qh_ref[h], k, _NT, preferred_element_type=F32) - slopes[h] * dist
        m_old = m_ref[h]
        m_new = jnp.maximum(m_old, jnp.max(s, axis=1, keepdims=True))
        alpha = jnp.exp(m_old - m_new)
        p = jnp.exp(s - m_new)
        l_ref[h] = alpha * l_ref[h] + jnp.sum(p, axis=1, keepdims=True)
        acc_ref[h] = alpha * acc_ref[h] + jnp.dot(p.astype(BF16), v, preferred_element_type=F32)
        m_ref[h] = m_new

    @pl.when(g == ng - 1)
    def _():
        lam = _diff_lambda(lam_ref, lam_init)
        heads = []
        for h in range(nh):
            on = acc_ref[h] / l_ref[h]
            heads.append(on[0:1] - lam * on[1:2])
        o = jnp.concatenate(heads, axis=0)
        o_ref[0] = _rms(o, g_ref[...]) * (1.0 - lam_init)


def _decode_diff(q, k_new, v_new, cache_k, cache_v, layer, page_table, lam_p, g_sub, lam_init):
    db, _, d = q.shape
    n_pool, page, nh = cache_k.shape[1], cache_k.shape[2], cache_k.shape[3]
    n_pages = page_table.shape[1]
    ck = cache_k.reshape(cache_k.shape[0], n_pool, page * nh, LANES)
    cv = cache_v.reshape(cache_v.shape[0], n_pool, page * nh, LANES)
    tok = pl.BlockSpec((1, 1, d), lambda b, g, pt: (b, 0, 0))
    kspecs = [pl.BlockSpec((1, 1, page * nh, LANES),
                           lambda b, g, pt, i=i: (layer, pt[b, g * PAGES_PER_STEP + i], 0, 0))
              for i in range(PAGES_PER_STEP)]
    grid_spec = pltpu.PrefetchScalarGridSpec(
        num_scalar_prefetch=1,
        grid=(db, n_pages // PAGES_PER_STEP),
        in_specs=[tok, tok, tok,
                  pl.BlockSpec(lam_p.shape, lambda b, g, pt: (0, 0)),
                  pl.BlockSpec((1, LANES), lambda b, g, pt: (0, 0))] + kspecs + kspecs,
        out_specs=pl.BlockSpec((1, nh, LANES), lambda b, g, pt: (b, 0, 0)),
        scratch_shapes=[pltpu.VMEM((nh, SUBLANES, LANES), BF16), pltpu.VMEM((nh, SUBLANES, LANES), F32),
                        pltpu.VMEM((nh, SUBLANES, 1), F32), pltpu.VMEM((nh, SUBLANES, 1), F32)],
    )
    out = pl.pallas_call(
        functools.partial(_dec_diff_kernel, past=n_pages * page, page=page, lam_init=lam_init,
                          slopes=_alibi_slopes(nh)),
        out_shape=jax.ShapeDtypeStruct((db, nh, LANES), F32),
        grid_spec=grid_spec,
        compiler_params=_params("arbitrary", "arbitrary"),
        name="diff_attention_decode",
    )(page_table, q, k_new, v_new, lam_p, g_sub, *([ck] * PAGES_PER_STEP), *([cv] * PAGES_PER_STEP))
    return out.reshape(db, 1, d)


def _dec_sb_kernel(pt_ref, q_ref, tri_ref, *rest):
    kt_refs = rest[0:PAGES_PER_STEP]
    vt_refs = rest[PAGES_PER_STEP:2 * PAGES_PER_STEP]
    o_ref, qb_ref, acc_ref, carry_ref = rest[2 * PAGES_PER_STEP:]
    g = pl.program_id(1)
    ng = pl.num_programs(1)
    nh, hd, page = qb_ref.shape

    @pl.when(g == 0)
    def _():
        q_cols = jnp.broadcast_to(q_ref[0], (LANES, nh * hd)).T
        qb_ref[...] = q_cols.reshape(nh, hd, page)
        carry_ref[...] = jnp.zeros_like(carry_ref)
        acc_ref[...] = jnp.zeros_like(acc_ref)

    qb = qb_ref[...]
    z = jnp.concatenate([jnp.sum(qb * r[0, 0], axis=1) for r in kt_refs], axis=1)
    l1m, lb = _sb_weights(z, None)
    c = carry_ref[...]
    a = jnp.exp(lb + (_suffix_sums(l1m, tri_ref[...]) + c))
    for i, r in enumerate(vt_refs):
        acc_ref[...] += a[:, i * page:(i + 1) * page][:, None, :] * r[0, 0]
    carry_ref[...] = c + jnp.sum(l1m, axis=1, keepdims=True)

    @pl.when(g == ng - 1)
    def _():
        o_ref[0] = jnp.sum(acc_ref[...].reshape(nh * hd, page).T, axis=0, keepdims=True)


def _decode_sb(q, cache_kt, cache_vt, layer, page_table):
    db, _, d = q.shape
    nh, hd, page = cache_kt.shape[2:]
    n_pages = page_table.shape[1]
    ntok = PAGES_PER_STEP * page
    tok = pl.BlockSpec((1, 1, d), lambda b, g, pt: (b, 0, 0))
    kspecs = [pl.BlockSpec((1, 1, nh, hd, page),
                           lambda b, g, pt, i=i: (layer, pt[b, n_pages - (g + 1) * PAGES_PER_STEP + i], 0, 0, 0))
              for i in range(PAGES_PER_STEP)]
    grid_spec = pltpu.PrefetchScalarGridSpec(
        num_scalar_prefetch=1,
        grid=(db, n_pages // PAGES_PER_STEP),
        in_specs=[tok, pl.BlockSpec((ntok, ntok), lambda b, g, pt: (0, 0))] + kspecs + kspecs,
        out_specs=tok,
        scratch_shapes=[pltpu.VMEM((nh, hd, page), F32), pltpu.VMEM((nh, hd, page), F32),
                        pltpu.VMEM((nh, 1), F32)],
    )
    return pl.pallas_call(
        _dec_sb_kernel,
        out_shape=jax.ShapeDtypeStruct((db, 1, d), F32),
        grid_spec=grid_spec,
        compiler_params=_params("arbitrary", "arbitrary"),
        name="sb_attention_decode",
    )(page_table, q, _suffix_matrix(ntok), *([cache_kt] * PAGES_PER_STEP), *([cache_vt] * PAGES_PER_STEP))


def _sample_ffn_kernel(x_ref, o_ref, wo_ref, gmp_ref, gtm_ref, gfp_ref, shf_ref, scf_ref, gtf_ref, gfo_ref,
                       wuv_ref, wug_ref, cwv_ref, cwg_ref, cbv_ref, cbg_ref,
                       h0v_ref, h0g_ref, h1v_ref, h1g_ref, wdn_ref,
                       xo_ref, uv_ref, ug_ref, x1_ref, h_ref, acc_ref):
    c = pl.program_id(0)

    @pl.when(c == 0)
    def _():
        y = jnp.dot(o_ref[...].astype(BF16), wo_ref[...], preferred_element_type=F32)
        x1 = x_ref[...] + gtm_ref[0] * _rms(y, gmp_ref[...])
        x1_ref[...] = x1
        h_ref[...] = (_rms(x1, gfp_ref[...]) * (1.0 + scf_ref[0]) + shf_ref[0]).astype(BF16)
        acc_ref[...] = jnp.zeros_like(acc_ref)

    h = h_ref[...]
    uv = jnp.dot(h, wuv_ref[...], preferred_element_type=F32)
    ug = jnp.dot(h, wug_ref[...], preferred_element_type=F32)
    uv_ref[...] = uv
    ug_ref[...] = ug
    val = cbv_ref[...] + h0v_ref[...] * cwv_ref[0:1] + h1v_ref[...] * cwv_ref[1:2] + uv * cwv_ref[2:3]
    gate = cbg_ref[...] + h0g_ref[...] * cwg_ref[0:1] + h1g_ref[...] * cwg_ref[1:2] + ug * cwg_ref[2:3]
    g = (_gelu_tanh(gate) * val).astype(BF16)
    acc_ref[...] += jnp.dot(g, wdn_ref[...], preferred_element_type=F32)

    @pl.when(c == pl.num_programs(0) - 1)
    def _():
        xo_ref[...] = x1_ref[...] + gtf_ref[0] * _rms(acc_ref[...], gfo_ref[...])


def _sample_ffn(x, o, mod, wo_bf, g_mix_post, g_ffn_pre, wup_bf, conv_w, conv_b, hist, wdn_bf, g_ffn_post, cw):
    r, d = x.shape
    ff = wdn_bf.shape[0]
    nc = ff // cw
    vec = pl.BlockSpec((1, d), lambda c: (0, 0))
    rowd = pl.BlockSpec((r, d), lambda c: (0, 0))

    def modc(chunk):
        return pl.BlockSpec((1, r, d), lambda c: (0, 0, chunk))

    def col(rows, blk0):
        return pl.BlockSpec((rows, cw), lambda c: (0, blk0 + c))

    return pl.pallas_call(
        _sample_ffn_kernel,
        out_shape=(jax.ShapeDtypeStruct((r, d), F32), jax.ShapeDtypeStruct((r, ff), F32),
                   jax.ShapeDtypeStruct((r, ff), F32)),
        grid=(nc,),
        in_specs=[rowd, rowd, pl.BlockSpec((d, d), lambda c: (0, 0)), vec, modc(2), vec, modc(3), modc(4), modc(5), vec,
                  col(d, 0), col(d, nc), col(CONV_W, 0), col(CONV_W, nc), col(1, 0), col(1, nc),
                  col(r, 0), col(r, nc), col(r, 2 * nc), col(r, 3 * nc),
                  pl.BlockSpec((cw, d), lambda c: (c, 0))],
        out_specs=(rowd, col(r, 0), col(r, 0)),
        scratch_shapes=[pltpu.VMEM((r, d), F32), pltpu.VMEM((r, d), BF16), pltpu.VMEM((r, d), F32)],
        compiler_params=_params("arbitrary"),
        name="attn_out_convffn_sample",
    )(x, o, wo_bf, g_mix_post, mod, g_ffn_pre, mod, mod, mod, g_ffn_post,
      wup_bf, wup_bf, conv_w, conv_w, conv_b, conv_b, hist, hist, hist, hist, wdn_bf)


def kernel(x_prompt, x_sample, cache_k_diff, cache_v_diff, cache_k_sb, cache_v_sb, state_conv, page_table,
           c_prompt, c_sample, w_ada, b_ada, g_mix_pre, g_mix_post, g_ffn_pre, g_ffn_post,
           w_qkv_diff, lambda_q1, lambda_k1, lambda_q2, lambda_k2, g_subln, w_o_diff,
           w_qkv_sb, w_o_sb, w_up, conv_w, conv_b, w_down):
    b, t_len, d = x_prompt.shape
    db = x_sample.shape[0]
    depth = w_ada.shape[0]
    ff = w_down.shape[1]
    nh_diff = d // LANES
    nh_sb = d // HEAD_LANES
    n_diff = w_qkv_diff.shape[0]
    n_sb = w_qkv_sb.shape[0]
    assert x_sample.shape[1] == 1 and t_len % ROW_TILE == 0 and ff % FF_CHUNK == 0

    mod = _modulation(jnp.concatenate([c_prompt, c_sample], axis=0), w_ada, b_ada)
    kfeat, qfeat = _alibi_tables(t_len, nh_diff)
    cache_kt_sb = jnp.transpose(cache_k_sb, (0, 1, 3, 4, 2))
    cache_vt_sb = jnp.transpose(cache_v_sb, (0, 1, 3, 4, 2))

    xp = x_prompt
    xs = x_sample.reshape(1, db, d)
    kd_p = vd_p = ks_p = vs_p = None
    outs = {n: [] for n in ("kd_s", "vd_s", "ks_s", "vs_s", "conv_p", "conv_s")}
    for layer in range(depth):
        j = layer // 2
        mod_p = mod[layer, :b].reshape(b, 1, 6 * d)
        mod_s = mod[layer, b:].reshape(1, db, 6 * d)
        g_pre = g_mix_pre[layer].reshape(1, d)
        is_diff = layer % 2 == 0
        w_qkv = (w_qkv_diff if is_diff else w_qkv_sb)[j]
        w_qkv_bf = w_qkv.astype(BF16)
        w_o = (w_o_diff if is_diff else w_o_sb)[j].astype(BF16)
        wup_bf = w_up[layer].astype(BF16)
        wdn_bf = w_down[layer].astype(BF16)

        qs, ksn, vsn = _qkv_sample(xs, mod_s, g_pre, w_qkv_bf)
        qs3, ksn3, vsn3 = (a.reshape(db, 1, d) for a in (qs, ksn, vsn))
        if is_diff:
            lam_init = 0.8 - 0.6 * math.exp(-0.3 * layer)
            lam_p = jnp.stack([lambda_q1[j], lambda_k1[j], lambda_q2[j], lambda_k2[j]])
            g_sub = g_subln[j].reshape(1, LANES)
            qp, kpb, vpb, kd_p, vd_p = _qkv_diff(xp, mod_p, g_pre, w_qkv_bf, kd_p, vd_p, j, n_diff, ROW_TILE)
            op = _diff_attention(qp, kpb, vpb, kfeat, qfeat, lam_p, g_sub, lam_init, ATTN_TILE_DIFF)
            os_ = _decode_diff(qs3, ksn3, vsn3, cache_k_diff, cache_v_diff, j, page_table, lam_p, g_sub, lam_init)
            outs["kd_s"].append(ksn.reshape(db, 1, nh_diff, LANES))
            outs["vd_s"].append(vsn.reshape(db, 1, nh_diff, LANES))
        else:
            wkt = w_qkv[:, d:2 * d].T.astype(BF16)
            wvt = w_qkv[:, 2 * d:3 * d].T.astype(BF16)
            qp, ktb, vtb, ks_p, vs_p = _qkv_sb(xp, mod_p, g_pre, w_qkv_bf[:, 0:d], wkt, wvt, ks_p, vs_p, j, n_sb,
                                               ROW_TILE)
            op = _sb_attention(qp, ktb, vtb, ATTN_TILE_SB)
            os_ = _decode_sb(qs3, cache_kt_sb, cache_vt_sb, j, page_table)
            outs["ks_s"].append(ksn.reshape(db, 1, nh_sb, HEAD_LANES))
            outs["vs_s"].append(vsn.reshape(db, 1, nh_sb, HEAD_LANES))

        gmp = g_mix_post[layer].reshape(1, d)
        gfp = g_ffn_pre[layer].reshape(1, d)
        gfo = g_ffn_post[layer].reshape(1, d)
        cb = conv_b[layer].reshape(1, 2 * ff)
        xp, hist_p = _post_ffn(xp, op, mod_p, w_o, gmp, gfp, wup_bf, conv_w[layer], cb, wdn_bf, gfo,
                               ROW_TILE, FF_CHUNK)
        hist_s = state_conv[layer].reshape(db, (CONV_W - 1) * 2 * ff)
        xs2, uv, ug = _sample_ffn(xs[0], os_.reshape(db, d), mod_s, w_o, gmp, gfp, wup_bf, conv_w[layer], cb,
                                  hist_s, wdn_bf, gfo, FF_CHUNK)
        xs = xs2.reshape(1, db, d)
        outs["conv_p"].append(hist_p)
        outs["conv_s"].append(jnp.stack([state_conv[layer][:, 1, :], jnp.concatenate([uv, ug], axis=1)], axis=1))

    st = {n: jnp.stack(v) for n, v in outs.items()}
    kd_p = kd_p.reshape(n_diff, b, t_len, nh_diff, LANES)
    vd_p = vd_p.reshape(n_diff, b, t_len, nh_diff, LANES)
    ks_p = jnp.transpose(ks_p.reshape(n_sb, b, nh_sb, HEAD_LANES, t_len), (0, 1, 4, 2, 3))
    vs_p = jnp.transpose(vs_p.reshape(n_sb, b, nh_sb, HEAD_LANES, t_len), (0, 1, 4, 2, 3))
    return (xp, xs.reshape(db, 1, d), kd_p, vd_p, ks_p, vs_p, st["conv_p"],
            st["kd_s"], st["vd_s"], st["ks_s"], st["vs_s"], st["conv_s"])
```

```python
import functools
import math

import numpy as np
import jax
import jax.numpy as jnp
from jax import lax
from jax.experimental import pallas as pl
from jax.experimental.pallas import tpu as pltpu

F32 = jnp.float32
BF16 = jnp.bfloat16
RMS_EPS = 1e-6
NEG_INF = -1e30
LANES = 128
SUBLANES = 8
HEAD_LANES = 64
CONV_W = 3
SB_EXIT = -104.0
VMEM_LIMIT = 56 * 1024 * 1024
_NT = (((1,), (1,)), ((), ()))

ROW_TILE = 512
ATTN_TILE_DIFF = 512
ATTN_TILE_SB = 256
FF_CHUNK = 256
PAGES_PER_STEP = 4
SB_DECODE_PAGES = 2


def _rms(x, g):
    return x * lax.rsqrt(jnp.mean(x * x, axis=-1, keepdims=True) + RMS_EPS) * g


def _params(*sem):
    return pltpu.CompilerParams(dimension_semantics=sem, vmem_limit_bytes=VMEM_LIMIT)


def _mod_kernel(c_ref, w_ref, b_ref, o_ref):
    c = c_ref[...]
    s = c * jax.nn.sigmoid(c)
    o_ref[0] = jnp.dot(s.astype(BF16), w_ref[0].astype(BF16), preferred_element_type=F32) + b_ref[0]


def _modulation(c_all, w_ada, b_ada):
    depth, d, n = w_ada.shape
    r = c_all.shape[0]
    tn = n // 4
    return pl.pallas_call(
        _mod_kernel,
        out_shape=jax.ShapeDtypeStruct((depth, r, n), F32),
        grid=(depth, n // tn),
        in_specs=[pl.BlockSpec((r, d), lambda l, j: (0, 0)),
                  pl.BlockSpec((1, d, tn), lambda l, j: (l, 0, j)),
                  pl.BlockSpec((1, 1, tn), lambda l, j: (l, 0, j))],
        out_specs=pl.BlockSpec((1, r, tn), lambda l, j: (l, 0, j)),
        compiler_params=_params("arbitrary", "arbitrary"),
        name="adaln_modulation",
    )(c_all, w_ada, b_ada.reshape(depth, 1, n))


def _mod_spec(rm, tm, d, chunk):
    if rm == 1:
        return pl.BlockSpec((1, 1, d), lambda b, t: (b, 0, chunk))
    return pl.BlockSpec((1, tm, d), lambda b, t: (b, t, chunk))


def _prenorm(x_ref, g_ref, sh_ref, sc_ref):
    return (_rms(x_ref[0], g_ref[...]) * (1.0 + sc_ref[0]) + sh_ref[0]).astype(BF16)


def _qkv_diff_kernel(x_ref, g_ref, sh_ref, sc_ref, w_ref, *rest, q_scale, aliased):
    if aliased:
        rest = rest[2:]
    q_ref, kb_ref, vb_ref, kf_ref, vf_ref = rest
    tm, d = x_ref.shape[1], x_ref.shape[2]
    nh = d // LANES
    h = _prenorm(x_ref, g_ref, sh_ref, sc_ref)
    q = jnp.dot(h, w_ref[:, 0:d], preferred_element_type=F32)
    q_ref[0] = (q * q_scale).astype(BF16)
    for src, bf_ref, f_ref in ((1, kb_ref, kf_ref), (2, vb_ref, vf_ref)):
        y = jnp.dot(h, w_ref[:, src * d:(src + 1) * d], preferred_element_type=F32)
        bf_ref[0] = y.astype(BF16)
        for hd in range(nh):
            f_ref[0, 0, pl.ds(hd, tm, stride=nh), :] = y[:, hd * LANES:(hd + 1) * LANES]


def _qkv_diff(x, mod, g_pre, w_bf, k_stack, v_stack, j, n_stack, tm):
    nb, tn, d = x.shape
    nh = d // LANES
    row = pl.BlockSpec((1, tm, d), lambda b, t: (b, t, 0))
    stack_spec = pl.BlockSpec((1, 1, tm * nh, LANES), lambda b, t: (j, b, t, 0))
    stack_shape = jax.ShapeDtypeStruct((n_stack, nb, tn * nh, LANES), F32)
    bfo = jax.ShapeDtypeStruct((nb, tn, d), BF16)
    aliased = k_stack is not None
    in_specs = [row, pl.BlockSpec((1, d), lambda b, t: (0, 0)),
                _mod_spec(1, tm, d, 0), _mod_spec(1, tm, d, 1),
                pl.BlockSpec((d, 3 * d), lambda b, t: (0, 0))]
    args = [x, g_pre, mod, mod, w_bf]
    aliases = {}
    if aliased:
        in_specs += [pl.BlockSpec(memory_space=pl.ANY)] * 2
        args += [k_stack, v_stack]
        aliases = {5: 3, 6: 4}
    return pl.pallas_call(
        functools.partial(_qkv_diff_kernel, q_scale=HEAD_LANES ** -0.5, aliased=aliased),
        out_shape=(bfo, bfo, bfo, stack_shape, stack_shape),
        grid=(nb, tn // tm),
        in_specs=in_specs,
        out_specs=(row, row, row, stack_spec, stack_spec),
        input_output_aliases=aliases,
        compiler_params=_params("arbitrary", "arbitrary"),
        name="prenorm_qkv_diff",
    )(*args)


def _qkv_sb_kernel(x_ref, g_ref, sh_ref, sc_ref, wq_ref, wkt_ref, wvt_ref, *rest, q_scale, aliased):
    if aliased:
        rest = rest[2:]
    q_ref, kb_ref, vb_ref, kf_ref, vf_ref = rest
    h = _prenorm(x_ref, g_ref, sh_ref, sc_ref)
    q_ref[0] = (jnp.dot(h, wq_ref[...], preferred_element_type=F32) * q_scale).astype(BF16)
    for wt_ref, bf_ref, f_ref in ((wkt_ref, kb_ref, kf_ref), (wvt_ref, vb_ref, vf_ref)):
        yt = lax.dot_general(wt_ref[...], h, _NT, preferred_element_type=F32)
        bf_ref[0] = yt.astype(BF16)
        f_ref[0, 0] = yt


def _qkv_sb(x, mod, g_pre, wq_bf, wkt_bf, wvt_bf, k_stack, v_stack, j, n_stack, tm):
    nb, tn, d = x.shape
    row = pl.BlockSpec((1, tm, d), lambda b, t: (b, t, 0))
    col = pl.BlockSpec((1, d, tm), lambda b, t: (b, 0, t))
    stack_spec = pl.BlockSpec((1, 1, d, tm), lambda b, t: (j, b, 0, t))
    stack_shape = jax.ShapeDtypeStruct((n_stack, nb, d, tn), F32)
    wspec = pl.BlockSpec((d, d), lambda b, t: (0, 0))
    aliased = k_stack is not None
    in_specs = [row, pl.BlockSpec((1, d), lambda b, t: (0, 0)),
                _mod_spec(1, tm, d, 0), _mod_spec(1, tm, d, 1), wspec, wspec, wspec]
    args = [x, g_pre, mod, mod, wq_bf, wkt_bf, wvt_bf]
    aliases = {}
    if aliased:
        in_specs += [pl.BlockSpec(memory_space=pl.ANY)] * 2
        args += [k_stack, v_stack]
        aliases = {7: 3, 8: 4}
    return pl.pallas_call(
        functools.partial(_qkv_sb_kernel, q_scale=HEAD_LANES ** -0.5, aliased=aliased),
        out_shape=(jax.ShapeDtypeStruct((nb, tn, d), BF16), jax.ShapeDtypeStruct((nb, d, tn), BF16),
                   jax.ShapeDtypeStruct((nb, d, tn), BF16), stack_shape, stack_shape),
        grid=(nb, tn // tm),
        in_specs=in_specs,
        out_specs=(row, col, col, stack_spec, stack_spec),
        input_output_aliases=aliases,
        compiler_params=_params("arbitrary", "arbitrary"),
        name="prenorm_qkv_sb",
    )(*args)


def _qkv_sample_kernel(x_ref, g_ref, sh_ref, sc_ref, w_ref, q_ref, k_ref, v_ref, *, q_scale):
    d = x_ref.shape[-1]
    h = _prenorm(x_ref, g_ref, sh_ref, sc_ref)
    q_ref[...] = jnp.dot(h, w_ref[:, 0:d], preferred_element_type=F32) * q_scale
    k_ref[...] = jnp.dot(h, w_ref[:, d:2 * d], preferred_element_type=F32)
    v_ref[...] = jnp.dot(h, w_ref[:, 2 * d:3 * d], preferred_element_type=F32)


def _qkv_sample(x, mod, g_pre, w_bf):
    _, r, d = x.shape
    o = jax.ShapeDtypeStruct((r, d), F32)
    ospec = pl.BlockSpec((r, d), lambda b, t: (0, 0))
    return pl.pallas_call(
        functools.partial(_qkv_sample_kernel, q_scale=HEAD_LANES ** -0.5),
        out_shape=(o, o, o),
        grid=(1, 1),
        in_specs=[pl.BlockSpec((1, r, d), lambda b, t: (0, 0, 0)), pl.BlockSpec((1, d), lambda b, t: (0, 0)),
                  _mod_spec(r, r, d, 0), _mod_spec(r, r, d, 1),
                  pl.BlockSpec((d, 3 * d), lambda b, t: (0, 0))],
        out_specs=(ospec, ospec, ospec),
        compiler_params=_params("arbitrary", "arbitrary"),
        name="prenorm_qkv_sample",
    )(x, g_pre, mod, mod, w_bf)


def _diff_lambda(lam_ref, lam_init):
    lp = lam_ref[...]
    a = jnp.sum(lp[0:1] * lp[1:2], axis=1, keepdims=True)
    b = jnp.sum(lp[2:3] * lp[3:4], axis=1, keepdims=True)
    return jnp.exp(a) - jnp.exp(b) + lam_init


def _diff_attn_kernel(q_ref, k_ref, v_ref, kf_ref, qf_ref, lam_ref, g_ref, o_ref, acc_ref, m_ref, sa_ref, sb_ref,
                      *, t, lam_init):
    qi = pl.program_id(2)
    lane = lax.broadcasted_iota(jnp.int32, (1, LANES), 1)
    low = lane < HEAD_LANES
    q = q_ref[0]
    qf = qf_ref[0]
    qmaps = (jnp.where(low, q, qf), jnp.where(low, qf, q))
    ones_blk = jnp.broadcast_to(jnp.where(lane == 0, 1.0, 0.0).astype(BF16), (t, LANES))
    acc_ref[...] = jnp.zeros_like(acc_ref)
    m_ref[...] = jnp.full_like(m_ref, NEG_INF)

    def scores(j, s_ref):
        off = pl.multiple_of(j * t, t)
        k = k_ref[0, pl.ds(off, t), :]
        kf = kf_ref[pl.ds(off, t), :]
        kmaps = (jnp.where(low, k, kf), jnp.where(low, kf, k))
        for c in range(2):
            s_ref[c] = lax.dot_general(qmaps[c], kmaps[c], _NT, preferred_element_type=F32)

    def accumulate(j, s_ref, masked):
        off = pl.multiple_of(j * t, t)
        v2 = jnp.concatenate([v_ref[0, pl.ds(off, t), :], ones_blk], axis=1)
        for c in range(2):
            s = s_ref[c]
            if masked:
                row = lax.broadcasted_iota(jnp.int32, (t, t), 0)
                col = lax.broadcasted_iota(jnp.int32, (t, t), 1)
                s = jnp.where(col <= row, s, NEG_INF)
            m_old = m_ref[c]
            m_new = jnp.maximum(m_old, jnp.max(s, axis=1, keepdims=True))
            p = jnp.exp(s - m_new)
            acc_ref[c] = jnp.exp(m_old - m_new) * acc_ref[c] + jnp.dot(p.astype(BF16), v2, preferred_element_type=F32)
            m_ref[c] = m_new

    scores(0, sa_ref)

    def pair(jj, carry):
        j0 = 2 * jj
        scores(j0 + 1, sb_ref)
        accumulate(j0, sa_ref, False)
        scores(j0 + 2, sa_ref)
        accumulate(j0 + 1, sb_ref, False)
        return carry

    lax.fori_loop(0, qi // 2, pair, 0)

    @pl.when(qi % 2 == 1)
    def _():
        scores(qi, sb_ref)
        accumulate(qi - 1, sa_ref, False)
        accumulate(qi, sb_ref, True)

    @pl.when(qi % 2 == 0)
    def _():
        accumulate(qi, sa_ref, True)

    lam = _diff_lambda(lam_ref, lam_init)
    a1 = acc_ref[0]
    a2 = acc_ref[1]
    o = a1[:, 0:LANES] / a1[:, LANES:LANES + 1] - lam * (a2[:, 0:LANES] / a2[:, LANES:LANES + 1])
    o_ref[0] = (_rms(o, g_ref[...]) * (1.0 - lam_init)).astype(BF16)


def _alibi_slopes(n_heads):
    return [2.0 ** (-8.0 * (i + 1) / n_heads) for i in range(n_heads)]


def _alibi_tables(t_len, n_heads):
    pos = np.arange(t_len)
    parts = np.stack([(pos >> 8) << 8, ((pos >> 4) & 15) << 4, pos & 15], axis=1).astype(np.float32)
    kf = np.zeros((t_len, LANES), np.float32)
    rem = jnp.asarray(_alibi_slopes(n_heads), F32)
    pieces = []
    for _ in range(3):
        p = rem.astype(BF16).astype(F32)
        pieces.append(p)
        rem = rem - p
    qf = jnp.zeros((n_heads, 1, LANES), F32)
    for base in (0, HEAD_LANES):
        for pi in range(3):
            for f in range(3):
                kf[:, base + 3 * pi + f] = parts[:, f]
                qf = qf.at[:, 0, base + 3 * pi + f].set(pieces[pi])
    return jnp.asarray(kf, BF16), qf.astype(BF16)


def _diff_attention(q, k, v, kfeat, qfeat, lam_p, g_sub, lam_init, t):
    b, t_len, d = q.shape
    nh = d // LANES
    qspec = pl.BlockSpec((1, t, LANES), lambda bi, h, i: (bi, i, h))
    kvspec = pl.BlockSpec((1, t_len, LANES), lambda bi, h, i: (bi, 0, h))
    return pl.pallas_call(
        functools.partial(_diff_attn_kernel, t=t, lam_init=lam_init),
        out_shape=jax.ShapeDtypeStruct((b, t_len, d), BF16),
        grid=(b, nh, t_len // t),
        in_specs=[qspec, kvspec, kvspec,
                  pl.BlockSpec((t_len, LANES), lambda bi, h, i: (0, 0)),
                  pl.BlockSpec((1, 1, LANES), lambda bi, h, i: (h, 0, 0)),
                  pl.BlockSpec(lam_p.shape, lambda bi, h, i: (0, 0)),
                  pl.BlockSpec((1, LANES), lambda bi, h, i: (0, 0))],
        out_specs=qspec,
        scratch_shapes=[pltpu.VMEM((2, t, 2 * LANES), F32), pltpu.VMEM((2, t, 1), F32),
                        pltpu.VMEM((2, t, t), F32), pltpu.VMEM((2, t, t), F32)],
        compiler_params=_params("arbitrary", "arbitrary", "arbitrary"),
        name="diff_attention_prompt",
    )(q, k, v, kfeat, qfeat, lam_p, g_sub)


def _sb_weights(z, valid):
    sp = jnp.maximum(z, 0.0) + jnp.log(1.0 + jnp.exp(-jnp.abs(z)))
    l1m = -sp
    if valid is not None:
        l1m = jnp.where(valid, l1m, 0.0)
    return l1m, z - sp


def _suffix_sums(l1m, tri):
    hi = l1m.astype(BF16)
    lo = (l1m - hi.astype(F32)).astype(BF16)
    return jnp.dot(hi, tri, preferred_element_type=F32) + jnp.dot(lo, tri, preferred_element_type=F32)


def _sb_attn_kernel(q_ref, kt_ref, vt_ref, tri_ref, o_ref, acc_ref, carry_ref, *, t):
    qi = pl.program_id(2)
    lane = lax.broadcasted_iota(jnp.int32, (1, LANES), 1)
    q = q_ref[0]
    tri = tri_ref[...]
    zero = jnp.zeros_like(q)
    qhs = (jnp.where(lane < HEAD_LANES, q, zero), jnp.where(lane >= HEAD_LANES, q, zero))
    acc_ref[...] = jnp.zeros_like(acc_ref)
    carry_ref[...] = jnp.zeros_like(carry_ref)

    def block(j, masked):
        off = pl.multiple_of(j * t, t)
        kt = kt_ref[0, :, pl.ds(off, t)]
        vt = vt_ref[0, :, pl.ds(off, t)]
        zs = [jnp.dot(qh, kt, preferred_element_type=F32) for qh in qhs]
        valid = None
        if masked:
            row = lax.broadcasted_iota(jnp.int32, (t, t), 0)
            col = lax.broadcasted_iota(jnp.int32, (t, t), 1)
            valid = col < row
        lws = [_sb_weights(z, valid) for z in zs]
        sums = [_suffix_sums(l1m, tri) for l1m, _ in lws]
        cs = [carry_ref[hh] for hh in range(2)]
        ws = []
        for hh in range(2):
            a = jnp.exp(lws[hh][1] + (sums[hh] + cs[hh]))
            if masked:
                a = jnp.where(valid, a, 0.0)
            ws.append(a.astype(BF16))
        for hh in range(2):
            acc_ref[hh] += lax.dot_general(ws[hh], vt, _NT, preferred_element_type=F32)
        c_new = [cs[hh] + jnp.sum(lws[hh][0], axis=1, keepdims=True) for hh in range(2)]
        for hh in range(2):
            carry_ref[hh] = c_new[hh]
        return jnp.max(jnp.maximum(c_new[0], c_new[1]))

    cmax = block(qi, True)

    def cond(st):
        return jnp.logical_and(st[0] >= 0, st[1] > SB_EXIT)

    def body(st):
        return st[0] - 1, block(st[0], False)

    lax.while_loop(cond, body, (qi - 1, cmax))
    o_ref[0] = jnp.where(lane < HEAD_LANES, acc_ref[0], acc_ref[1]).astype(BF16)


def _suffix_matrix(n):
    j = np.arange(n)[:, None]
    s = np.arange(n)[None, :]
    return jnp.asarray((j > s).astype(np.float32), BF16)


def _sb_attention(q, kt, vt, t):
    b, t_len, d = q.shape
    qspec = pl.BlockSpec((1, t, LANES), lambda bi, h, i: (bi, i, h))
    kvspec = pl.BlockSpec((1, LANES, t_len), lambda bi, h, i: (bi, h, 0))
    return pl.pallas_call(
        functools.partial(_sb_attn_kernel, t=t),
        out_shape=jax.ShapeDtypeStruct((b, t_len, d), BF16),
        grid=(b, d // LANES, t_len // t),
        in_specs=[qspec, kvspec, kvspec, pl.BlockSpec((t, t), lambda bi, h, i: (0, 0))],
        out_specs=qspec,
        scratch_shapes=[pltpu.VMEM((2, t, LANES), F32), pltpu.VMEM((2, t, 1), F32)],
        compiler_params=_params("arbitrary", "arbitrary", "arbitrary"),
        name="sb_attention_prompt",
    )(q, kt, vt, _suffix_matrix(t))


def _gelu_tanh(x):
    return 0.5 * x * (1.0 + jnp.tanh(math.sqrt(2.0 / math.pi) * (x + 0.044715 * (x * x * x))))


def _post_ffn_kernel(x_ref, o_ref, wo_ref, gmp_ref, gtm_ref, gfp_ref, shf_ref, scf_ref, gtf_ref,
                     wup_ref, cw_ref, cb_ref, wdn_ref, gfo_ref,
                     xo_ref, hist_ref, carry_ref, ext_ref, acc_ref, g_ref, *, tm, cw, ff):
    ti = pl.program_id(1)
    nt = pl.num_programs(1)

    @pl.when(ti == 0)
    def _():
        carry_ref[...] = jnp.zeros_like(carry_ref)

    y = jnp.dot(o_ref[0], wo_ref[...], preferred_element_type=F32)
    x1 = x_ref[0] + gtm_ref[0] * _rms(y, gmp_ref[...])
    h = (_rms(x1, gfp_ref[...]) * (1.0 + scf_ref[0]) + shf_ref[0]).astype(BF16)

    def conv(off):
        u = jnp.dot(h, wup_ref[:, off:off + cw], preferred_element_type=F32)
        ext_ref[0:SUBLANES, :] = carry_ref[:, off:off + cw]
        ext_ref[SUBLANES:SUBLANES + tm, :] = u
        y = (cb_ref[:, off:off + cw]
             + ext_ref[SUBLANES - 2:SUBLANES - 2 + tm, :] * cw_ref[0:1, off:off + cw]
             + ext_ref[SUBLANES - 1:SUBLANES - 1 + tm, :] * cw_ref[1:2, off:off + cw]
             + u * cw_ref[2:3, off:off + cw])
        carry_ref[:, off:off + cw] = u[tm - SUBLANES:tm, :]
        return y

    for c in range(ff // cw):
        val = conv(c * cw)
        gate = conv(ff + c * cw)
        g_ref[:, c * cw:(c + 1) * cw] = (_gelu_tanh(gate) * val).astype(BF16)
    acc_ref[...] = jnp.dot(g_ref[...], wdn_ref[...], preferred_element_type=F32)

    xo_ref[0] = x1 + gtf_ref[0] * _rms(acc_ref[...], gfo_ref[...])

    @pl.when(ti == nt - 1)
    def _():
        hist_ref[0] = carry_ref[SUBLANES - (CONV_W - 1):SUBLANES, :]


def _post_ffn(x, o, mod, wo_bf, g_mix_post, g_ffn_pre, wup_bf, conv_w, conv_b, wdn_bf, g_ffn_post, tm, cw):
    nb, tn, d = x.shape
    ff = wdn_bf.shape[0]
    rm = mod.shape[1]
    row = pl.BlockSpec((1, tm, d), lambda b, t: (b, t, 0))
    vec = pl.BlockSpec((1, d), lambda b, t: (0, 0))

    def full(a):
        return pl.BlockSpec(a.shape, lambda b, t: (0,) * a.ndim)

    return pl.pallas_call(
        functools.partial(_post_ffn_kernel, tm=tm, cw=cw, ff=ff),
        out_shape=(jax.ShapeDtypeStruct((nb, tn, d), F32),
                   jax.ShapeDtypeStruct((nb, CONV_W - 1, 2 * ff), F32)),
        grid=(nb, tn // tm),
        in_specs=[row, row, full(wo_bf), vec, _mod_spec(rm, tm, d, 2), vec,
                  _mod_spec(rm, tm, d, 3), _mod_spec(rm, tm, d, 4), _mod_spec(rm, tm, d, 5),
                  full(wup_bf), full(conv_w), full(conv_b), full(wdn_bf), vec],
        out_specs=(row, pl.BlockSpec((1, CONV_W - 1, 2 * ff), lambda b, t: (b, 0, 0))),
        scratch_shapes=[pltpu.VMEM((SUBLANES, 2 * ff), F32), pltpu.VMEM((tm + SUBLANES, cw), F32),
                        pltpu.VMEM((tm, d), F32), pltpu.VMEM((tm, ff), BF16)],
        compiler_params=_params("arbitrary", "arbitrary"),
        name="attn_out_convffn_prompt",
    )(x, o, wo_bf, g_mix_post, mod, g_ffn_pre, mod, mod, mod, wup_bf, conv_w, conv_b, wdn_bf, g_ffn_post)


def _head_rows(q_row, n_rows):
    d = q_row.shape[-1]
    r = lax.broadcasted_iota(jnp.int32, (n_rows, d), 0)
    grp = lax.broadcasted_iota(jnp.int32, (n_rows, d), 1) // HEAD_LANES
    return jnp.where(grp == r, jnp.broadcast_to(q_row, (n_rows, d)), 0.0)


def _dec_diff_kernel(pt_ref, q_ref, kn_ref, vn_ref, lam_ref, g_ref, *rest, past, page, lam_init, slopes):
    k_refs = rest[0:PAGES_PER_STEP]
    v_refs = rest[PAGES_PER_STEP:2 * PAGES_PER_STEP]
    o_ref, qh_ref, acc_ref, m_ref, l_ref = rest[2 * PAGES_PER_STEP:]
    g = pl.program_id(1)
    ng = pl.num_programs(1)
    nh = len(slopes)
    ntok = PAGES_PER_STEP * page

    @pl.when(g == 0)
    def _():
        for h in range(nh):
            sl = slice(h * LANES, (h + 1) * LANES)
            qh = _head_rows(q_ref[0][:, sl], SUBLANES)
            qh_ref[h] = qh.astype(BF16)
            m_ref[h] = jnp.sum(qh * kn_ref[0][:, sl], axis=1, keepdims=True)
            l_ref[h] = jnp.ones((SUBLANES, 1), F32)
            acc_ref[h] = jnp.broadcast_to(vn_ref[0][:, sl], (SUBLANES, LANES))

    pos = g * ntok + lax.broadcasted_iota(jnp.int32, (1, ntok), 1)
    dist = (past - pos).astype(F32)
    ss = []
    for h in range(nh):
        k = jnp.concatenate([r[0, 0, pl.ds(h, page, stride=nh), :] for r in k_refs], axis=0).astype(BF16)
        ss.append(lax.dot_general(qh_ref[h], k, _NT, preferred_element_type=F32) - slopes[h] * dist)
    ps = []
    for h in range(nh):
        s = ss[h]
        m_old = m_ref[h]
        m_new = jnp.maximum(m_old, jnp.max(s, axis=1, keepdims=True))
        alpha = jnp.exp(m_old - m_new)
        p = jnp.exp(s - m_new)
        l_ref[h] = alpha * l_ref[h] + jnp.sum(p, axis=1, keepdims=True)
        m_ref[h] = m_new
        ps.append((alpha, p.astype(BF16)))
    for h in range(nh):
        v = jnp.concatenate([r[0, 0, pl.ds(h, page, stride=nh), :] for r in v_refs], axis=0).astype(BF16)
        acc_ref[h] = ps[h][0] * acc_ref[h] + jnp.dot(ps[h][1], v, preferred_element_type=F32)

    @pl.when(g == ng - 1)
    def _():
        lam = _diff_lambda(lam_ref, lam_init)
        heads = []
        for h in range(nh):
            on = acc_ref[h] / l_ref[h]
            heads.append(on[0:1] - lam * on[1:2])
        o = jnp.concatenate(heads, axis=0)
        o_ref[0] = _rms(o, g_ref[...]) * (1.0 - lam_init)


def _decode_diff(q, k_new, v_new, cache_k, cache_v, layer, page_table, lam_p, g_sub, lam_init):
    db, _, d = q.shape
    n_pool, page, nh = cache_k.shape[1], cache_k.shape[2], cache_k.shape[3]
    n_pages = page_table.shape[1]
    ck = cache_k.reshape(cache_k.shape[0], n_pool, page * nh, LANES)
    cv = cache_v.reshape(cache_v.shape[0], n_pool, page * nh, LANES)
    tok = pl.BlockSpec((1, 1, d), lambda b, g, pt: (b, 0, 0))
    kspecs = [pl.BlockSpec((1, 1, page * nh, LANES),
                           lambda b, g, pt, i=i: (layer, pt[b, g * PAGES_PER_STEP + i], 0, 0))
              for i in range(PAGES_PER_STEP)]
    grid_spec = pltpu.PrefetchScalarGridSpec(
        num_scalar_prefetch=1,
        grid=(db, n_pages // PAGES_PER_STEP),
        in_specs=[tok, tok, tok,
                  pl.BlockSpec(lam_p.shape, lambda b, g, pt: (0, 0)),
                  pl.BlockSpec((1, LANES), lambda b, g, pt: (0, 0))] + kspecs + kspecs,
        out_specs=pl.BlockSpec((1, nh, LANES), lambda b, g, pt: (b, 0, 0)),
        scratch_shapes=[pltpu.VMEM((nh, SUBLANES, LANES), BF16), pltpu.VMEM((nh, SUBLANES, LANES), F32),
                        pltpu.VMEM((nh, SUBLANES, 1), F32), pltpu.VMEM((nh, SUBLANES, 1), F32)],
    )
    out = pl.pallas_call(
        functools.partial(_dec_diff_kernel, past=n_pages * page, page=page, lam_init=lam_init,
                          slopes=_alibi_slopes(nh)),
        out_shape=jax.ShapeDtypeStruct((db, nh, LANES), F32),
        grid_spec=grid_spec,
        compiler_params=_params("arbitrary", "arbitrary"),
        name="diff_attention_decode",
    )(page_table, q, k_new, v_new, lam_p, g_sub, *([ck] * PAGES_PER_STEP), *([cv] * PAGES_PER_STEP))
    return out.reshape(db, 1, d)


def _dec_sb_kernel(pt_ref, q_ref, tri_ref, kt_hbm, vt_hbm, o_ref, kbuf, vbuf, sem, qb_ref, acc_ref, carry_ref,
                   *, layer, n_pages):
    b = pl.program_id(0)
    nh, hd, page = qb_ref.shape
    ng = n_pages // SB_DECODE_PAGES

    def copies(g, slot):
        cps = []
        for i in range(SB_DECODE_PAGES):
            pg = pt_ref[b, n_pages - (g + 1) * SB_DECODE_PAGES + i]
            cps.append(pltpu.make_async_copy(kt_hbm.at[layer, pg], kbuf.at[slot, i], sem.at[0, slot, i]))
            cps.append(pltpu.make_async_copy(vt_hbm.at[layer, pg], vbuf.at[slot, i], sem.at[1, slot, i]))
        return cps

    for cp in copies(0, 0):
        cp.start()

    q_cols = jnp.broadcast_to(q_ref[0], (LANES, nh * hd)).T
    qb_ref[...] = q_cols.reshape(nh, hd, page)
    carry_ref[...] = jnp.zeros_like(carry_ref)
    acc_ref[...] = jnp.zeros_like(acc_ref)

    def cond(st):
        return jnp.logical_and(st[0] < ng, st[1] > SB_EXIT)

    def body(st):
        g = st[0]
        slot = lax.rem(g, 2)

        @pl.when(g + 1 < ng)
        def _():
            for cp in copies(g + 1, 1 - slot):
                cp.start()

        for cp in copies(g, slot):
            cp.wait()
        qb = qb_ref[...]
        z = jnp.concatenate([jnp.sum(qb * kbuf[slot, i], axis=1) for i in range(SB_DECODE_PAGES)], axis=1)
        l1m, lb = _sb_weights(z, None)
        c = carry_ref[...]
        a = jnp.exp(lb + (_suffix_sums(l1m, tri_ref[...]) + c))
        for i in range(SB_DECODE_PAGES):
            acc_ref[...] += a[:, i * page:(i + 1) * page][:, None, :] * vbuf[slot, i]
        c_new = c + jnp.sum(l1m, axis=1, keepdims=True)
        carry_ref[...] = c_new
        return g + 1, jnp.max(c_new)

    g_end, _ = lax.while_loop(cond, body, (jnp.int32(0), jnp.float32(0.0)))

    @pl.when(g_end < ng)
    def _():
        for cp in copies(g_end, lax.rem(g_end, 2)):
            cp.wait()

    o_ref[0] = jnp.sum(acc_ref[...].reshape(nh * hd, page).T, axis=0, keepdims=True)


def _decode_sb(q, cache_kt, cache_vt, layer, page_table):
    db, _, d = q.shape
    nh, hd, page = cache_kt.shape[2:]
    n_pages = page_table.shape[1]
    ntok = SB_DECODE_PAGES * page
    tok = pl.BlockSpec((1, 1, d), lambda b, pt: (b, 0, 0))
    grid_spec = pltpu.PrefetchScalarGridSpec(
        num_scalar_prefetch=1,
        grid=(db,),
        in_specs=[tok, pl.BlockSpec((ntok, ntok), lambda b, pt: (0, 0)),
                  pl.BlockSpec(memory_space=pl.ANY), pl.BlockSpec(memory_space=pl.ANY)],
        out_specs=tok,
        scratch_shapes=[pltpu.VMEM((2, SB_DECODE_PAGES, nh, hd, page), F32),
                        pltpu.VMEM((2, SB_DECODE_PAGES, nh, hd, page), F32),
                        pltpu.SemaphoreType.DMA((2, 2, SB_DECODE_PAGES)),
                        pltpu.VMEM((nh, hd, page), F32), pltpu.VMEM((nh, hd, page), F32),
                        pltpu.VMEM((nh, 1), F32)],
    )
    return pl.pallas_call(
        functools.partial(_dec_sb_kernel, layer=layer, n_pages=n_pages),
        out_shape=jax.ShapeDtypeStruct((db, 1, d), F32),
        grid_spec=grid_spec,
        compiler_params=_params("arbitrary"),
        name="sb_attention_decode",
    )(page_table, q, _suffix_matrix(ntok), cache_kt, cache_vt)


def _sample_ffn_kernel(x_ref, o_ref, wo_ref, gmp_ref, gtm_ref, gfp_ref, shf_ref, scf_ref, gtf_ref, gfo_ref,
                       wuv_ref, wug_ref, cwv_ref, cwg_ref, cbv_ref, cbg_ref,
                       h0v_ref, h0g_ref, h1v_ref, h1g_ref, wdn_ref,
                       xo_ref, uv_ref, ug_ref, x1_ref, h_ref, acc_ref):
    c = pl.program_id(0)

    @pl.when(c == 0)
    def _():
        y = jnp.dot(o_ref[...].astype(BF16), wo_ref[...], preferred_element_type=F32)
        x1 = x_ref[...] + gtm_ref[0] * _rms(y, gmp_ref[...])
        x1_ref[...] = x1
        h_ref[...] = (_rms(x1, gfp_ref[...]) * (1.0 + scf_ref[0]) + shf_ref[0]).astype(BF16)
        acc_ref[...] = jnp.zeros_like(acc_ref)

    h = h_ref[...]
    uv = jnp.dot(h, wuv_ref[...], preferred_element_type=F32)
    ug = jnp.dot(h, wug_ref[...], preferred_element_type=F32)
    uv_ref[...] = uv
    ug_ref[...] = ug
    val = cbv_ref[...] + h0v_ref[...] * cwv_ref[0:1] + h1v_ref[...] * cwv_ref[1:2] + uv * cwv_ref[2:3]
    gate = cbg_ref[...] + h0g_ref[...] * cwg_ref[0:1] + h1g_ref[...] * cwg_ref[1:2] + ug * cwg_ref[2:3]
    g = (_gelu_tanh(gate) * val).astype(BF16)
    acc_ref[...] += jnp.dot(g, wdn_ref[...], preferred_element_type=F32)

    @pl.when(c == pl.num_programs(0) - 1)
    def _():
        xo_ref[...] = x1_ref[...] + gtf_ref[0] * _rms(acc_ref[...], gfo_ref[...])


def _sample_ffn(x, o, mod, wo_bf, g_mix_post, g_ffn_pre, wup_bf, conv_w, conv_b, hist, wdn_bf, g_ffn_post, cw):
    r, d = x.shape
    ff = wdn_bf.shape[0]
    nc = ff // cw
    vec = pl.BlockSpec((1, d), lambda c: (0, 0))
    rowd = pl.BlockSpec((r, d), lambda c: (0, 0))

    def modc(chunk):
        return pl.BlockSpec((1, r, d), lambda c: (0, 0, chunk))

    def col(rows, blk0):
        return pl.BlockSpec((rows, cw), lambda c: (0, blk0 + c))

    return pl.pallas_call(
        _sample_ffn_kernel,
        out_shape=(jax.ShapeDtypeStruct((r, d), F32), jax.ShapeDtypeStruct((r, ff), F32),
                   jax.ShapeDtypeStruct((r, ff), F32)),
        grid=(nc,),
        in_specs=[rowd, rowd, pl.BlockSpec((d, d), lambda c: (0, 0)), vec, modc(2), vec, modc(3), modc(4), modc(5), vec,
                  col(d, 0), col(d, nc), col(CONV_W, 0), col(CONV_W, nc), col(1, 0), col(1, nc),
                  col(r, 0), col(r, nc), col(r, 2 * nc), col(r, 3 * nc),
                  pl.BlockSpec((cw, d), lambda c: (c, 0))],
        out_specs=(rowd, col(r, 0), col(r, 0)),
        scratch_shapes=[pltpu.VMEM((r, d), F32), pltpu.VMEM((r, d), BF16), pltpu.VMEM((r, d), F32)],
        compiler_params=_params("arbitrary"),
        name="attn_out_convffn_sample",
    )(x, o, wo_bf, g_mix_post, mod, g_ffn_pre, mod, mod, mod, g_ffn_post,
      wup_bf, wup_bf, conv_w, conv_w, conv_b, conv_b, hist, hist, hist, hist, wdn_bf)


def kernel(x_prompt, x_sample, cache_k_diff, cache_v_diff, cache_k_sb, cache_v_sb, state_conv, page_table,
           c_prompt, c_sample, w_ada, b_ada, g_mix_pre, g_mix_post, g_ffn_pre, g_ffn_post,
           w_qkv_diff, lambda_q1, lambda_k1, lambda_q2, lambda_k2, g_subln, w_o_diff,
           w_qkv_sb, w_o_sb, w_up, conv_w, conv_b, w_down):
    b, t_len, d = x_prompt.shape
    db = x_sample.shape[0]
    depth = w_ada.shape[0]
    ff = w_down.shape[1]
    nh_diff = d // LANES
    nh_sb = d // HEAD_LANES
    n_diff = w_qkv_diff.shape[0]
    n_sb = w_qkv_sb.shape[0]
    assert x_sample.shape[1] == 1 and t_len % ROW_TILE == 0 and ff % FF_CHUNK == 0

    mod = _modulation(jnp.concatenate([c_prompt, c_sample], axis=0), w_ada, b_ada)
    kfeat, qfeat = _alibi_tables(t_len, nh_diff)
    cache_kt_sb = jnp.transpose(cache_k_sb, (0, 1, 3, 4, 2))
    cache_vt_sb = jnp.transpose(cache_v_sb, (0, 1, 3, 4, 2))

    xp = x_prompt
    xs = x_sample.reshape(1, db, d)
    kd_p = vd_p = ks_p = vs_p = None
    outs = {n: [] for n in ("kd_s", "vd_s", "ks_s", "vs_s", "conv_p", "conv_s")}
    for layer in range(depth):
        j = layer // 2
        mod_p = mod[layer, :b].reshape(b, 1, 6 * d)
        mod_s = mod[layer, b:].reshape(1, db, 6 * d)
        g_pre = g_mix_pre[layer].reshape(1, d)
        is_diff = layer % 2 == 0
        w_qkv = (w_qkv_diff if is_diff else w_qkv_sb)[j]
        w_qkv_bf = w_qkv.astype(BF16)
        w_o = (w_o_diff if is_diff else w_o_sb)[j].astype(BF16)
        wup_bf = w_up[layer].astype(BF16)
        wdn_bf = w_down[layer].astype(BF16)

        qs, ksn, vsn = _qkv_sample(xs, mod_s, g_pre, w_qkv_bf)
        qs3, ksn3, vsn3 = (a.reshape(db, 1, d) for a in (qs, ksn, vsn))
        if is_diff:
            lam_init = 0.8 - 0.6 * math.exp(-0.3 * layer)
            lam_p = jnp.stack([lambda_q1[j], lambda_k1[j], lambda_q2[j], lambda_k2[j]])
            g_sub = g_subln[j].reshape(1, LANES)
            qp, kpb, vpb, kd_p, vd_p = _qkv_diff(xp, mod_p, g_pre, w_qkv_bf, kd_p, vd_p, j, n_diff, ROW_TILE)
            op = _diff_attention(qp, kpb, vpb, kfeat, qfeat, lam_p, g_sub, lam_init, ATTN_TILE_DIFF)
            os_ = _decode_diff(qs3, ksn3, vsn3, cache_k_diff, cache_v_diff, j, page_table, lam_p, g_sub, lam_init)
            outs["kd_s"].append(ksn.reshape(db, 1, nh_diff, LANES))
            outs["vd_s"].append(vsn.reshape(db, 1, nh_diff, LANES))
        else:
            wkt = w_qkv[:, d:2 * d].T.astype(BF16)
            wvt = w_qkv[:, 2 * d:3 * d].T.astype(BF16)
            qp, ktb, vtb, ks_p, vs_p = _qkv_sb(xp, mod_p, g_pre, w_qkv_bf[:, 0:d], wkt, wvt, ks_p, vs_p, j, n_sb,
                                               ROW_TILE)
            op = _sb_attention(qp, ktb, vtb, ATTN_TILE_SB)
            os_ = _decode_sb(qs3, cache_kt_sb, cache_vt_sb, j, page_table)
            outs["ks_s"].append(ksn.reshape(db, 1, nh_sb, HEAD_LANES))
            outs["vs_s"].append(vsn.reshape(db, 1, nh_sb, HEAD_LANES))

        gmp = g_mix_post[layer].reshape(1, d)
        gfp = g_ffn_pre[layer].reshape(1, d)
        gfo = g_ffn_post[layer].reshape(1, d)
        cb = conv_b[layer].reshape(1, 2 * ff)
        xp, hist_p = _post_ffn(xp, op, mod_p, w_o, gmp, gfp, wup_bf, conv_w[layer], cb, wdn_bf, gfo,
                               ROW_TILE, FF_CHUNK)
        hist_s = state_conv[layer].reshape(db, (CONV_W - 1) * 2 * ff)
        xs2, uv, ug = _sample_ffn(xs[0], os_.reshape(db, d), mod_s, w_o, gmp, gfp, wup_bf, conv_w[layer], cb,
                                  hist_s, wdn_bf, gfo, FF_CHUNK)
        xs = xs2.reshape(1, db, d)
        outs["conv_p"].append(hist_p)
        outs["conv_s"].append(jnp.stack([state_conv[layer][:, 1, :], jnp.concatenate([uv, ug], axis=1)], axis=1))

    st = {n: jnp.stack(v) for n, v in outs.items()}
    kd_p = kd_p.reshape(n_diff, b, t_len, nh_diff, LANES)
    vd_p = vd_p.reshape(n_diff, b, t_len, nh_diff, LANES)
    ks_p = jnp.transpose(ks_p.reshape(n_sb, b, nh_sb, HEAD_LANES, t_len), (0, 1, 4, 2, 3))
    vs_p = jnp.transpose(vs_p.reshape(n_sb, b, nh_sb, HEAD_LANES, t_len), (0, 1, 4, 2, 3))
    return (xp, xs.reshape(db, 1, d), kd_p, vd_p, ks_p, vs_p, st["conv_p"],
            st["kd_s"], st["vd_s"], st["ks_s"], st["vs_s"], st["conv_s"])
```

```python
import functools
import math

import numpy as np
import jax
import jax.numpy as jnp
from jax import lax
from jax.experimental import pallas as pl
from jax.experimental.pallas import tpu as pltpu

F32 = jnp.float32
BF16 = jnp.bfloat16
RMS_EPS = 1e-6
NEG_INF = -1e30
LANES = 128
SUBLANES = 8
HEAD_LANES = 64
CONV_W = 3
SB_EXIT = -104.0
SOFTMAX_EXIT = -104.0
NORM_SLACK = 1.02
VMEM_LIMIT = 56 * 1024 * 1024
_NT = (((1,), (1,)), ((), ()))

ROW_TILE = 512
ATTN_TILE_DIFF = 512
ATTN_TILE_SB = 256
FF_CHUNK = 256
PAGES_PER_STEP = 8
SB_DECODE_PAGES = 2


def _rms(x, g):
    return x * lax.rsqrt(jnp.mean(x * x, axis=-1, keepdims=True) + RMS_EPS) * g


def _params(*sem):
    return pltpu.CompilerParams(dimension_semantics=sem, vmem_limit_bytes=VMEM_LIMIT)


def _mod_kernel(c_ref, w_ref, b_ref, o_ref):
    c = c_ref[...]
    s = c * jax.nn.sigmoid(c)
    o_ref[0] = jnp.dot(s.astype(BF16), w_ref[0].astype(BF16), preferred_element_type=F32) + b_ref[0]


def _modulation(c_all, w_ada, b_ada):
    depth, d, n = w_ada.shape
    r = c_all.shape[0]
    tn = n // 4
    return pl.pallas_call(
        _mod_kernel,
        out_shape=jax.ShapeDtypeStruct((depth, r, n), F32),
        grid=(depth, n // tn),
        in_specs=[pl.BlockSpec((r, d), lambda l, j: (0, 0)),
                  pl.BlockSpec((1, d, tn), lambda l, j: (l, 0, j)),
                  pl.BlockSpec((1, 1, tn), lambda l, j: (l, 0, j))],
        out_specs=pl.BlockSpec((1, r, tn), lambda l, j: (l, 0, j)),
        compiler_params=_params("arbitrary", "arbitrary"),
        name="adaln_modulation",
    )(c_all, w_ada, b_ada.reshape(depth, 1, n))


def _mod_spec(rm, tm, d, chunk):
    if rm == 1:
        return pl.BlockSpec((1, 1, d), lambda b, t: (b, 0, chunk))
    return pl.BlockSpec((1, tm, d), lambda b, t: (b, t, chunk))


def _prenorm(x_ref, g_ref, sh_ref, sc_ref):
    return (_rms(x_ref[0], g_ref[...]) * (1.0 + sc_ref[0]) + sh_ref[0]).astype(BF16)


def _qkv_diff_kernel(x_ref, g_ref, sh_ref, sc_ref, w_ref, *rest, q_scale, aliased):
    if aliased:
        rest = rest[2:]
    q_ref, kb_ref, vb_ref, kf_ref, vf_ref = rest
    tm, d = x_ref.shape[1], x_ref.shape[2]
    nh = d // LANES
    h = _prenorm(x_ref, g_ref, sh_ref, sc_ref)
    q = jnp.dot(h, w_ref[:, 0:d], preferred_element_type=F32)
    q_ref[0] = (q * q_scale).astype(BF16)
    for src, bf_ref, f_ref in ((1, kb_ref, kf_ref), (2, vb_ref, vf_ref)):
        y = jnp.dot(h, w_ref[:, src * d:(src + 1) * d], preferred_element_type=F32)
        bf_ref[0] = y.astype(BF16)
        for hd in range(nh):
            f_ref[0, 0, pl.ds(hd, tm, stride=nh), :] = y[:, hd * LANES:(hd + 1) * LANES]


def _qkv_diff(x, mod, g_pre, w_bf, k_stack, v_stack, j, n_stack, tm):
    nb, tn, d = x.shape
    nh = d // LANES
    row = pl.BlockSpec((1, tm, d), lambda b, t: (b, t, 0))
    stack_spec = pl.BlockSpec((1, 1, tm * nh, LANES), lambda b, t: (j, b, t, 0))
    stack_shape = jax.ShapeDtypeStruct((n_stack, nb, tn * nh, LANES), F32)
    bfo = jax.ShapeDtypeStruct((nb, tn, d), BF16)
    aliased = k_stack is not None
    in_specs = [row, pl.BlockSpec((1, d), lambda b, t: (0, 0)),
                _mod_spec(1, tm, d, 0), _mod_spec(1, tm, d, 1),
                pl.BlockSpec((d, 3 * d), lambda b, t: (0, 0))]
    args = [x, g_pre, mod, mod, w_bf]
    aliases = {}
    if aliased:
        in_specs += [pl.BlockSpec(memory_space=pl.ANY)] * 2
        args += [k_stack, v_stack]
        aliases = {5: 3, 6: 4}
    return pl.pallas_call(
        functools.partial(_qkv_diff_kernel, q_scale=HEAD_LANES ** -0.5, aliased=aliased),
        out_shape=(bfo, bfo, bfo, stack_shape, stack_shape),
        grid=(nb, tn // tm),
        in_specs=in_specs,
        out_specs=(row, row, row, stack_spec, stack_spec),
        input_output_aliases=aliases,
        compiler_params=_params("arbitrary", "arbitrary"),
        name="prenorm_qkv_diff",
    )(*args)


def _qkv_sb_kernel(x_ref, g_ref, sh_ref, sc_ref, wq_ref, wkt_ref, wvt_ref, *rest, q_scale, aliased):
    if aliased:
        rest = rest[2:]
    q_ref, kb_ref, vb_ref, kf_ref, vf_ref = rest
    h = _prenorm(x_ref, g_ref, sh_ref, sc_ref)
    q_ref[0] = (jnp.dot(h, wq_ref[...], preferred_element_type=F32) * q_scale).astype(BF16)
    for wt_ref, bf_ref, f_ref in ((wkt_ref, kb_ref, kf_ref), (wvt_ref, vb_ref, vf_ref)):
        yt = lax.dot_general(wt_ref[...], h, _NT, preferred_element_type=F32)
        bf_ref[0] = yt.astype(BF16)
        f_ref[0, 0] = yt


def _qkv_sb(x, mod, g_pre, wq_bf, wkt_bf, wvt_bf, k_stack, v_stack, j, n_stack, tm):
    nb, tn, d = x.shape
    row = pl.BlockSpec((1, tm, d), lambda b, t: (b, t, 0))
    col = pl.BlockSpec((1, d, tm), lambda b, t: (b, 0, t))
    stack_spec = pl.BlockSpec((1, 1, d, tm), lambda b, t: (j, b, 0, t))
    stack_shape = jax.ShapeDtypeStruct((n_stack, nb, d, tn), F32)
    wspec = pl.BlockSpec((d, d), lambda b, t: (0, 0))
    aliased = k_stack is not None
    in_specs = [row, pl.BlockSpec((1, d), lambda b, t: (0, 0)),
                _mod_spec(1, tm, d, 0), _mod_spec(1, tm, d, 1), wspec, wspec, wspec]
    args = [x, g_pre, mod, mod, wq_bf, wkt_bf, wvt_bf]
    aliases = {}
    if aliased:
        in_specs += [pl.BlockSpec(memory_space=pl.ANY)] * 2
        args += [k_stack, v_stack]
        aliases = {7: 3, 8: 4}
    return pl.pallas_call(
        functools.partial(_qkv_sb_kernel, q_scale=HEAD_LANES ** -0.5, aliased=aliased),
        out_shape=(jax.ShapeDtypeStruct((nb, tn, d), BF16), jax.ShapeDtypeStruct((nb, d, tn), BF16),
                   jax.ShapeDtypeStruct((nb, d, tn), BF16), stack_shape, stack_shape),
        grid=(nb, tn // tm),
        in_specs=in_specs,
        out_specs=(row, col, col, stack_spec, stack_spec),
        input_output_aliases=aliases,
        compiler_params=_params("arbitrary", "arbitrary"),
        name="prenorm_qkv_sb",
    )(*args)


def _qkv_sample_kernel(x_ref, g_ref, sh_ref, sc_ref, w_ref, q_ref, k_ref, v_ref, *, q_scale):
    d = x_ref.shape[-1]
    h = _prenorm(x_ref, g_ref, sh_ref, sc_ref)
    q_ref[...] = jnp.dot(h, w_ref[:, 0:d], preferred_element_type=F32) * q_scale
    k_ref[...] = jnp.dot(h, w_ref[:, d:2 * d], preferred_element_type=F32)
    v_ref[...] = jnp.dot(h, w_ref[:, 2 * d:3 * d], preferred_element_type=F32)


def _qkv_sample(x, mod, g_pre, w_bf):
    _, r, d = x.shape
    o = jax.ShapeDtypeStruct((r, d), F32)
    ospec = pl.BlockSpec((r, d), lambda b, t: (0, 0))
    return pl.pallas_call(
        functools.partial(_qkv_sample_kernel, q_scale=HEAD_LANES ** -0.5),
        out_shape=(o, o, o),
        grid=(1, 1),
        in_specs=[pl.BlockSpec((1, r, d), lambda b, t: (0, 0, 0)), pl.BlockSpec((1, d), lambda b, t: (0, 0)),
                  _mod_spec(r, r, d, 0), _mod_spec(r, r, d, 1),
                  pl.BlockSpec((d, 3 * d), lambda b, t: (0, 0))],
        out_specs=(ospec, ospec, ospec),
        compiler_params=_params("arbitrary", "arbitrary"),
        name="prenorm_qkv_sample",
    )(x, g_pre, mod, mod, w_bf)


def _diff_lambda(lam_ref, lam_init):
    lp = lam_ref[...]
    a = jnp.sum(lp[0:1] * lp[1:2], axis=1, keepdims=True)
    b = jnp.sum(lp[2:3] * lp[3:4], axis=1, keepdims=True)
    return jnp.exp(a) - jnp.exp(b) + lam_init


def _half_norms_sq(x, ind):
    return jnp.dot(x * x, ind, preferred_element_type=F32)


def _diff_attn_kernel(slope_ref, q_ref, k_ref, v_ref, kf_ref, qf_ref, lam_ref, g_ref, o_ref,
                      acc_ref, m_ref, sa_ref, sb_ref, kn_ref, *, t, lam_init):
    h = pl.program_id(1)
    qi = pl.program_id(2)
    n_keys = k_ref.shape[1]
    lane = lax.broadcasted_iota(jnp.int32, (1, LANES), 1)
    low = lane < HEAD_LANES
    q = q_ref[0]
    qf = qf_ref[0]
    qmaps = (jnp.where(low, q, qf), jnp.where(low, qf, q))
    ones_blk = jnp.broadcast_to(jnp.where(lane == 0, 1.0, 0.0).astype(BF16), (t, LANES))
    half = lax.broadcasted_iota(jnp.int32, (LANES, LANES), 0) // HEAD_LANES
    ind = jnp.where(half == lax.broadcasted_iota(jnp.int32, (LANES, LANES), 1), 1.0, 0.0).astype(BF16)
    acc_ref[...] = jnp.zeros_like(acc_ref)
    m_ref[...] = jnp.full_like(m_ref, NEG_INF)

    @pl.when(qi == 0)
    def _():
        def chunk(i, best):
            kk = k_ref[0, pl.ds(pl.multiple_of(i * t, t), t), :]
            return jnp.maximum(best, jnp.max(_half_norms_sq(kk, ind), axis=0, keepdims=True))

        kn_ref[...] = lax.fori_loop(0, n_keys // t, chunk, jnp.zeros((1, LANES), F32))

    def scores(j, s_ref):
        off = pl.multiple_of(j * t, t)
        k = k_ref[0, pl.ds(off, t), :]
        kf = kf_ref[pl.ds(off, t), :]
        kmaps = (jnp.where(low, k, kf), jnp.where(low, kf, k))
        for c in range(2):
            s_ref[c] = lax.dot_general(qmaps[c], kmaps[c], _NT, preferred_element_type=F32)

    def accumulate(j, s_ref, masked):
        off = pl.multiple_of(j * t, t)
        v2 = jnp.concatenate([v_ref[0, pl.ds(off, t), :], ones_blk], axis=1)
        for c in range(2):
            s = s_ref[c]
            if masked:
                row = lax.broadcasted_iota(jnp.int32, (t, t), 0)
                col = lax.broadcasted_iota(jnp.int32, (t, t), 1)
                s = jnp.where(col <= row, s, NEG_INF)
            m_old = m_ref[c]
            m_new = jnp.maximum(m_old, jnp.max(s, axis=1, keepdims=True))
            p = jnp.exp(s - m_new)
            acc_ref[c] = jnp.exp(m_old - m_new) * acc_ref[c] + jnp.dot(p.astype(BF16), v2, preferred_element_type=F32)
            m_ref[c] = m_new

    qn = jnp.max(_half_norms_sq(q, ind), axis=0, keepdims=True)
    ub = jnp.max(jnp.sqrt(qn * kn_ref[...])) * NORM_SLACK
    p_star = (qi * t).astype(F32) - (2.0 * ub - SOFTMAX_EXIT) / slope_ref[h]
    j_min = jnp.clip(jnp.floor((p_star + 1.0) / t - 1.0), 0.0, qi.astype(F32)).astype(jnp.int32)
    n = qi - j_min

    scores(j_min, sa_ref)

    def pair(jj, carry):
        j0 = j_min + 2 * jj
        scores(j0 + 1, sb_ref)
        accumulate(j0, sa_ref, False)
        scores(j0 + 2, sa_ref)
        accumulate(j0 + 1, sb_ref, False)
        return carry

    lax.fori_loop(0, n // 2, pair, 0)

    @pl.when(n % 2 == 1)
    def _():
        scores(qi, sb_ref)
        accumulate(qi - 1, sa_ref, False)
        accumulate(qi, sb_ref, True)

    @pl.when(n % 2 == 0)
    def _():
        accumulate(qi, sa_ref, True)

    lam = _diff_lambda(lam_ref, lam_init)
    a1 = acc_ref[0]
    a2 = acc_ref[1]
    o = a1[:, 0:LANES] / a1[:, LANES:LANES + 1] - lam * (a2[:, 0:LANES] / a2[:, LANES:LANES + 1])
    o_ref[0] = (_rms(o, g_ref[...]) * (1.0 - lam_init)).astype(BF16)


def _alibi_slopes(n_heads):
    return [2.0 ** (-8.0 * (i + 1) / n_heads) for i in range(n_heads)]


def _alibi_tables(t_len, n_heads):
    pos = np.arange(t_len)
    parts = np.stack([(pos >> 8) << 8, ((pos >> 4) & 15) << 4, pos & 15], axis=1).astype(np.float32)
    kf = np.zeros((t_len, LANES), np.float32)
    rem = jnp.asarray(_alibi_slopes(n_heads), F32)
    pieces = []
    for _ in range(3):
        p = rem.astype(BF16).astype(F32)
        pieces.append(p)
        rem = rem - p
    qf = jnp.zeros((n_heads, 1, LANES), F32)
    for base in (0, HEAD_LANES):
        for pi in range(3):
            for f in range(3):
                kf[:, base + 3 * pi + f] = parts[:, f]
                qf = qf.at[:, 0, base + 3 * pi + f].set(pieces[pi])
    return jnp.asarray(kf, BF16), qf.astype(BF16)


def _diff_attention(q, k, v, kfeat, qfeat, lam_p, g_sub, lam_init, t):
    b, t_len, d = q.shape
    nh = d // LANES
    qspec = pl.BlockSpec((1, t, LANES), lambda bi, h, i, sl: (bi, i, h))
    kvspec = pl.BlockSpec((1, t_len, LANES), lambda bi, h, i, sl: (bi, 0, h))
    grid_spec = pltpu.PrefetchScalarGridSpec(
        num_scalar_prefetch=1,
        grid=(b, nh, t_len // t),
        in_specs=[qspec, kvspec, kvspec,
                  pl.BlockSpec((t_len, LANES), lambda bi, h, i, sl: (0, 0)),
                  pl.BlockSpec((1, 1, LANES), lambda bi, h, i, sl: (h, 0, 0)),
                  pl.BlockSpec(lam_p.shape, lambda bi, h, i, sl: (0, 0)),
                  pl.BlockSpec((1, LANES), lambda bi, h, i, sl: (0, 0))],
        out_specs=qspec,
        scratch_shapes=[pltpu.VMEM((2, t, 2 * LANES), F32), pltpu.VMEM((2, t, 1), F32),
                        pltpu.VMEM((2, t, t), F32), pltpu.VMEM((2, t, t), F32), pltpu.VMEM((1, LANES), F32)],
    )
    return pl.pallas_call(
        functools.partial(_diff_attn_kernel, t=t, lam_init=lam_init),
        out_shape=jax.ShapeDtypeStruct((b, t_len, d), BF16),
        grid_spec=grid_spec,
        compiler_params=_params("arbitrary", "arbitrary", "arbitrary"),
        name="diff_attention_prompt",
    )(jnp.asarray(_alibi_slopes(nh), F32), q, k, v, kfeat, qfeat, lam_p, g_sub)


def _sb_weights(z, valid):
    sp = jnp.maximum(z, 0.0) + jnp.log(1.0 + jnp.exp(-jnp.abs(z)))
    l1m = -sp
    if valid is not None:
        l1m = jnp.where(valid, l1m, 0.0)
    return l1m, z - sp


def _suffix_sums(l1m, tri):
    hi = l1m.astype(BF16)
    lo = (l1m - hi.astype(F32)).astype(BF16)
    return jnp.dot(hi, tri, preferred_element_type=F32) + jnp.dot(lo, tri, preferred_element_type=F32)


def _sb_attn_kernel(q_ref, kt_ref, vt_ref, tri_ref, o_ref, acc_ref, carry_ref, *, t):
    qi = pl.program_id(2)
    lane = lax.broadcasted_iota(jnp.int32, (1, LANES), 1)
    q = q_ref[0]
    tri = tri_ref[...]
    zero = jnp.zeros_like(q)
    qhs = (jnp.where(lane < HEAD_LANES, q, zero), jnp.where(lane >= HEAD_LANES, q, zero))
    acc_ref[...] = jnp.zeros_like(acc_ref)
    carry_ref[...] = jnp.zeros_like(carry_ref)

    def block(j, masked):
        off = pl.multiple_of(j * t, t)
        kt = kt_ref[0, :, pl.ds(off, t)]
        vt = vt_ref[0, :, pl.ds(off, t)]
        zs = [jnp.dot(qh, kt, preferred_element_type=F32) for qh in qhs]
        valid = None
        if masked:
            row = lax.broadcasted_iota(jnp.int32, (t, t), 0)
            col = lax.broadcasted_iota(jnp.int32, (t, t), 1)
            valid = col < row
        lws = [_sb_weights(z, valid) for z in zs]
        sums = [_suffix_sums(l1m, tri) for l1m, _ in lws]
        cs = [carry_ref[hh] for hh in range(2)]
        ws = []
        for hh in range(2):
            a = jnp.exp(lws[hh][1] + (sums[hh] + cs[hh]))
            if masked:
                a = jnp.where(valid, a, 0.0)
            ws.append(a.astype(BF16))
        for hh in range(2):
            acc_ref[hh] += lax.dot_general(ws[hh], vt, _NT, preferred_element_type=F32)
        c_new = [cs[hh] + jnp.sum(lws[hh][0], axis=1, keepdims=True) for hh in range(2)]
        for hh in range(2):
            carry_ref[hh] = c_new[hh]
        return jnp.max(jnp.maximum(c_new[0], c_new[1]))

    cmax = block(qi, True)

    def cond(st):
        return jnp.logical_and(st[0] >= 0, st[1] > SB_EXIT)

    def body(st):
        return st[0] - 1, block(st[0], False)

    lax.while_loop(cond, body, (qi - 1, cmax))
    o_ref[0] = jnp.where(lane < HEAD_LANES, acc_ref[0], acc_ref[1]).astype(BF16)


def _suffix_matrix(n):
    j = np.arange(n)[:, None]
    s = np.arange(n)[None, :]
    return jnp.asarray((j > s).astype(np.float32), BF16)


def _sb_attention(q, kt, vt, t):
    b, t_len, d = q.shape
    qspec = pl.BlockSpec((1, t, LANES), lambda bi, h, i: (bi, i, h))
    kvspec = pl.BlockSpec((1, LANES, t_len), lambda bi, h, i: (bi, h, 0))
    return pl.pallas_call(
        functools.partial(_sb_attn_kernel, t=t),
        out_shape=jax.ShapeDtypeStruct((b, t_len, d), BF16),
        grid=(b, d // LANES, t_len // t),
        in_specs=[qspec, kvspec, kvspec, pl.BlockSpec((t, t), lambda bi, h, i: (0, 0))],
        out_specs=qspec,
        scratch_shapes=[pltpu.VMEM((2, t, LANES), F32), pltpu.VMEM((2, t, 1), F32)],
        compiler_params=_params("arbitrary", "arbitrary", "arbitrary"),
        name="sb_attention_prompt",
    )(q, kt, vt, _suffix_matrix(t))


def _gelu_tanh(x):
    return 0.5 * x * (1.0 + jnp.tanh(math.sqrt(2.0 / math.pi) * (x + 0.044715 * (x * x * x))))


def _post_ffn_kernel(x_ref, o_ref, wo_ref, gmp_ref, gtm_ref, gfp_ref, shf_ref, scf_ref, gtf_ref,
                     wup_ref, cw_ref, cb_ref, wdn_ref, gfo_ref,
                     xo_ref, hist_ref, carry_ref, ext_ref, acc_ref, g_ref, *, tm, cw, ff):
    ti = pl.program_id(1)
    nt = pl.num_programs(1)

    @pl.when(ti == 0)
    def _():
        carry_ref[...] = jnp.zeros_like(carry_ref)

    y = jnp.dot(o_ref[0], wo_ref[...], preferred_element_type=F32)
    x1 = x_ref[0] + gtm_ref[0] * _rms(y, gmp_ref[...])
    h = (_rms(x1, gfp_ref[...]) * (1.0 + scf_ref[0]) + shf_ref[0]).astype(BF16)

    def conv(off):
        u = jnp.dot(h, wup_ref[:, off:off + cw], preferred_element_type=F32)
        ext_ref[0:SUBLANES, :] = carry_ref[:, off:off + cw]
        ext_ref[SUBLANES:SUBLANES + tm, :] = u
        y = (cb_ref[:, off:off + cw]
             + ext_ref[SUBLANES - 2:SUBLANES - 2 + tm, :] * cw_ref[0:1, off:off + cw]
             + ext_ref[SUBLANES - 1:SUBLANES - 1 + tm, :] * cw_ref[1:2, off:off + cw]
             + u * cw_ref[2:3, off:off + cw])
        carry_ref[:, off:off + cw] = u[tm - SUBLANES:tm, :]
        return y

    for c in range(ff // cw):
        val = conv(c * cw)
        gate = conv(ff + c * cw)
        g_ref[:, c * cw:(c + 1) * cw] = (_gelu_tanh(gate) * val).astype(BF16)
    acc_ref[...] = jnp.dot(g_ref[...], wdn_ref[...], preferred_element_type=F32)

    xo_ref[0] = x1 + gtf_ref[0] * _rms(acc_ref[...], gfo_ref[...])

    @pl.when(ti == nt - 1)
    def _():
        hist_ref[0] = carry_ref[SUBLANES - (CONV_W - 1):SUBLANES, :]


def _post_ffn(x, o, mod, wo_bf, g_mix_post, g_ffn_pre, wup_bf, conv_w, conv_b, wdn_bf, g_ffn_post, tm, cw):
    nb, tn, d = x.shape
    ff = wdn_bf.shape[0]
    rm = mod.shape[1]
    row = pl.BlockSpec((1, tm, d), lambda b, t: (b, t, 0))
    vec = pl.BlockSpec((1, d), lambda b, t: (0, 0))

    def full(a):
        return pl.BlockSpec(a.shape, lambda b, t: (0,) * a.ndim)

    return pl.pallas_call(
        functools.partial(_post_ffn_kernel, tm=tm, cw=cw, ff=ff),
        out_shape=(jax.ShapeDtypeStruct((nb, tn, d), F32),
                   jax.ShapeDtypeStruct((nb, CONV_W - 1, 2 * ff), F32)),
        grid=(nb, tn // tm),
        in_specs=[row, row, full(wo_bf), vec, _mod_spec(rm, tm, d, 2), vec,
                  _mod_spec(rm, tm, d, 3), _mod_spec(rm, tm, d, 4), _mod_spec(rm, tm, d, 5),
                  full(wup_bf), full(conv_w), full(conv_b), full(wdn_bf), vec],
        out_specs=(row, pl.BlockSpec((1, CONV_W - 1, 2 * ff), lambda b, t: (b, 0, 0))),
        scratch_shapes=[pltpu.VMEM((SUBLANES, 2 * ff), F32), pltpu.VMEM((tm + SUBLANES, cw), F32),
                        pltpu.VMEM((tm, d), F32), pltpu.VMEM((tm, ff), BF16)],
        compiler_params=_params("arbitrary", "arbitrary"),
        name="attn_out_convffn_prompt",
    )(x, o, wo_bf, g_mix_post, mod, g_ffn_pre, mod, mod, mod, wup_bf, conv_w, conv_b, wdn_bf, g_ffn_post)


def _head_rows(q_row, n_rows):
    d = q_row.shape[-1]
    r = lax.broadcasted_iota(jnp.int32, (n_rows, d), 0)
    grp = lax.broadcasted_iota(jnp.int32, (n_rows, d), 1) // HEAD_LANES
    return jnp.where(grp == r, jnp.broadcast_to(q_row, (n_rows, d)), 0.0)


def _dec_diff_kernel(pt_ref, q_ref, kn_ref, vn_ref, lam_ref, g_ref, *rest, past, page, lam_init, slopes):
    k_refs = rest[0:PAGES_PER_STEP]
    v_refs = rest[PAGES_PER_STEP:2 * PAGES_PER_STEP]
    o_ref, qh_ref, acc_ref, m_ref, l_ref = rest[2 * PAGES_PER_STEP:]
    g = pl.program_id(1)
    ng = pl.num_programs(1)
    nh = len(slopes)
    ntok = PAGES_PER_STEP * page

    @pl.when(g == 0)
    def _():
        for h in range(nh):
            sl = slice(h * LANES, (h + 1) * LANES)
            qh = _head_rows(q_ref[0][:, sl], SUBLANES)
            qh_ref[h] = qh.astype(BF16)
            m_ref[h] = jnp.sum(qh * kn_ref[0][:, sl], axis=1, keepdims=True)
            l_ref[h] = jnp.ones((SUBLANES, 1), F32)
            acc_ref[h] = jnp.broadcast_to(vn_ref[0][:, sl], (SUBLANES, LANES))

    pos = g * ntok + lax.broadcasted_iota(jnp.int32, (1, ntok), 1)
    dist = (past - pos).astype(F32)
    ss = []
    for h in range(nh):
        k = jnp.concatenate([r[0, 0, pl.ds(h, page, stride=nh), :] for r in k_refs], axis=0).astype(BF16)
        ss.append(lax.dot_general(qh_ref[h], k, _NT, preferred_element_type=F32) - slopes[h] * dist)
    ps = []
    for h in range(nh):
        s = ss[h]
        m_old = m_ref[h]
        m_new = jnp.maximum(m_old, jnp.max(s, axis=1, keepdims=True))
        alpha = jnp.exp(m_old - m_new)
        p = jnp.exp(s - m_new)
        l_ref[h] = alpha * l_ref[h] + jnp.sum(p, axis=1, keepdims=True)
        m_ref[h] = m_new
        ps.append((alpha, p.astype(BF16)))
    for h in range(nh):
        v = jnp.concatenate([r[0, 0, pl.ds(h, page, stride=nh), :] for r in v_refs], axis=0).astype(BF16)
        acc_ref[h] = ps[h][0] * acc_ref[h] + jnp.dot(ps[h][1], v, preferred_element_type=F32)

    @pl.when(g == ng - 1)
    def _():
        lam = _diff_lambda(lam_ref, lam_init)
        heads = []
        for h in range(nh):
            on = acc_ref[h] / l_ref[h]
            heads.append(on[0:1] - lam * on[1:2])
        o = jnp.concatenate(heads, axis=0)
        o_ref[0] = _rms(o, g_ref[...]) * (1.0 - lam_init)


def _decode_diff(q, k_new, v_new, cache_k, cache_v, layer, page_table, lam_p, g_sub, lam_init):
    db, _, d = q.shape
    n_pool, page, nh = cache_k.shape[1], cache_k.shape[2], cache_k.shape[3]
    n_pages = page_table.shape[1]
    ck = cache_k.reshape(cache_k.shape[0], n_pool, page * nh, LANES)
    cv = cache_v.reshape(cache_v.shape[0], n_pool, page * nh, LANES)
    tok = pl.BlockSpec((1, 1, d), lambda b, g, pt: (b, 0, 0))
    kspecs = [pl.BlockSpec((1, 1, page * nh, LANES),
                           lambda b, g, pt, i=i: (layer, pt[b, g * PAGES_PER_STEP + i], 0, 0))
              for i in range(PAGES_PER_STEP)]
    grid_spec = pltpu.PrefetchScalarGridSpec(
        num_scalar_prefetch=1,
        grid=(db, n_pages // PAGES_PER_STEP),
        in_specs=[tok, tok, tok,
                  pl.BlockSpec(lam_p.shape, lambda b, g, pt: (0, 0)),
                  pl.BlockSpec((1, LANES), lambda b, g, pt: (0, 0))] + kspecs + kspecs,
        out_specs=pl.BlockSpec((1, nh, LANES), lambda b, g, pt: (b, 0, 0)),
        scratch_shapes=[pltpu.VMEM((nh, SUBLANES, LANES), BF16), pltpu.VMEM((nh, SUBLANES, LANES), F32),
                        pltpu.VMEM((nh, SUBLANES, 1), F32), pltpu.VMEM((nh, SUBLANES, 1), F32)],
    )
    out = pl.pallas_call(
        functools.partial(_dec_diff_kernel, past=n_pages * page, page=page, lam_init=lam_init,
                          slopes=_alibi_slopes(nh)),
        out_shape=jax.ShapeDtypeStruct((db, nh, LANES), F32),
        grid_spec=grid_spec,
        compiler_params=_params("arbitrary", "arbitrary"),
        name="diff_attention_decode",
    )(page_table, q, k_new, v_new, lam_p, g_sub, *([ck] * PAGES_PER_STEP), *([cv] * PAGES_PER_STEP))
    return out.reshape(db, 1, d)


def _dec_sb_kernel(pt_ref, q_ref, tri_ref, kt_hbm, vt_hbm, o_ref, kbuf, vbuf, sem, qb_ref, acc_ref, carry_ref,
                   *, layer, n_pages):
    b = pl.program_id(0)
    nh, hd, page = qb_ref.shape
    ng = n_pages // SB_DECODE_PAGES

    def copies(g, slot):
        cps = []
        for i in range(SB_DECODE_PAGES):
            pg = pt_ref[b, n_pages - (g + 1) * SB_DECODE_PAGES + i]
            cps.append(pltpu.make_async_copy(kt_hbm.at[layer, pg], kbuf.at[slot, i], sem.at[0, slot, i]))
            cps.append(pltpu.make_async_copy(vt_hbm.at[layer, pg], vbuf.at[slot, i], sem.at[1, slot, i]))
        return cps

    for cp in copies(0, 0):
        cp.start()

    q_cols = jnp.broadcast_to(q_ref[0], (LANES, nh * hd)).T
    qb_ref[...] = q_cols.reshape(nh, hd, page)
    carry_ref[...] = jnp.zeros_like(carry_ref)
    acc_ref[...] = jnp.zeros_like(acc_ref)

    def cond(st):
        return jnp.logical_and(st[0] < ng, st[1] > SB_EXIT)

    def body(st):
        g = st[0]
        slot = lax.rem(g, 2)

        @pl.when(g + 1 < ng)
        def _():
            for cp in copies(g + 1, 1 - slot):
                cp.start()

        for cp in copies(g, slot):
            cp.wait()
        qb = qb_ref[...]
        z = jnp.concatenate([jnp.sum(qb * kbuf[slot, i], axis=1) for i in range(SB_DECODE_PAGES)], axis=1)
        l1m, lb = _sb_weights(z, None)
        c = carry_ref[...]
        a = jnp.exp(lb + (_suffix_sums(l1m, tri_ref[...]) + c))
        for i in range(SB_DECODE_PAGES):
            acc_ref[...] += a[:, i * page:(i + 1) * page][:, None, :] * vbuf[slot, i]
        c_new = c + jnp.sum(l1m, axis=1, keepdims=True)
        carry_ref[...] = c_new
        return g + 1, jnp.max(c_new)

    g_end, _ = lax.while_loop(cond, body, (jnp.int32(0), jnp.float32(0.0)))

    @pl.when(g_end < ng)
    def _():
        for cp in copies(g_end, lax.rem(g_end, 2)):
            cp.wait()

    o_ref[0] = jnp.sum(acc_ref[...].reshape(nh * hd, page).T, axis=0, keepdims=True)


def _decode_sb(q, cache_kt, cache_vt, layer, page_table):
    db, _, d = q.shape
    nh, hd, page = cache_kt.shape[2:]
    n_pages = page_table.shape[1]
    ntok = SB_DECODE_PAGES * page
    tok = pl.BlockSpec((1, 1, d), lambda b, pt: (b, 0, 0))
    grid_spec = pltpu.PrefetchScalarGridSpec(
        num_scalar_prefetch=1,
        grid=(db,),
        in_specs=[tok, pl.BlockSpec((ntok, ntok), lambda b, pt: (0, 0)),
                  pl.BlockSpec(memory_space=pl.ANY), pl.BlockSpec(memory_space=pl.ANY)],
        out_specs=tok,
        scratch_shapes=[pltpu.VMEM((2, SB_DECODE_PAGES, nh, hd, page), F32),
                        pltpu.VMEM((2, SB_DECODE_PAGES, nh, hd, page), F32),
                        pltpu.SemaphoreType.DMA((2, 2, SB_DECODE_PAGES)),
                        pltpu.VMEM((nh, hd, page), F32), pltpu.VMEM((nh, hd, page), F32),
                        pltpu.VMEM((nh, 1), F32)],
    )
    return pl.pallas_call(
        functools.partial(_dec_sb_kernel, layer=layer, n_pages=n_pages),
        out_shape=jax.ShapeDtypeStruct((db, 1, d), F32),
        grid_spec=grid_spec,
        compiler_params=_params("arbitrary"),
        name="sb_attention_decode",
    )(page_table, q, _suffix_matrix(ntok), cache_kt, cache_vt)


def _sample_ffn_kernel(x_ref, o_ref, wo_ref, gmp_ref, gtm_ref, gfp_ref, shf_ref, scf_ref, gtf_ref, gfo_ref,
                       wuv_ref, wug_ref, cwv_ref, cwg_ref, cbv_ref, cbg_ref,
                       h0v_ref, h0g_ref, h1v_ref, h1g_ref, wdn_ref,
                       xo_ref, uv_ref, ug_ref, x1_ref, h_ref, acc_ref):
    c = pl.program_id(0)

    @pl.when(c == 0)
    def _():
        y = jnp.dot(o_ref[...].astype(BF16), wo_ref[...], preferred_element_type=F32)
        x1 = x_ref[...] + gtm_ref[0] * _rms(y, gmp_ref[...])
        x1_ref[...] = x1
        h_ref[...] = (_rms(x1, gfp_ref[...]) * (1.0 + scf_ref[0]) + shf_ref[0]).astype(BF16)
        acc_ref[...] = jnp.zeros_like(acc_ref)

    h = h_ref[...]
    uv = jnp.dot(h, wuv_ref[...], preferred_element_type=F32)
    ug = jnp.dot(h, wug_ref[...], preferred_element_type=F32)
    uv_ref[...] = uv
    ug_ref[...] = ug
    val = cbv_ref[...] + h0v_ref[...] * cwv_ref[0:1] + h1v_ref[...] * cwv_ref[1:2] + uv * cwv_ref[2:3]
    gate = cbg_ref[...] + h0g_ref[...] * cwg_ref[0:1] + h1g_ref[...] * cwg_ref[1:2] + ug * cwg_ref[2:3]
    g = (_gelu_tanh(gate) * val).astype(BF16)
    acc_ref[...] += jnp.dot(g, wdn_ref[...], preferred_element_type=F32)

    @pl.when(c == pl.num_programs(0) - 1)
    def _():
        xo_ref[...] = x1_ref[...] + gtf_ref[0] * _rms(acc_ref[...], gfo_ref[...])


def _sample_ffn(x, o, mod, wo_bf, g_mix_post, g_ffn_pre, wup_bf, conv_w, conv_b, hist, wdn_bf, g_ffn_post, cw):
    r, d = x.shape
    ff = wdn_bf.shape[0]
    nc = ff // cw
    vec = pl.BlockSpec((1, d), lambda c: (0, 0))
    rowd = pl.BlockSpec((r, d), lambda c: (0, 0))

    def modc(chunk):
        return pl.BlockSpec((1, r, d), lambda c: (0, 0, chunk))

    def col(rows, blk0):
        return pl.BlockSpec((rows, cw), lambda c: (0, blk0 + c))

    return pl.pallas_call(
        _sample_ffn_kernel,
        out_shape=(jax.ShapeDtypeStruct((r, d), F32), jax.ShapeDtypeStruct((r, ff), F32),
                   jax.ShapeDtypeStruct((r, ff), F32)),
        grid=(nc,),
        in_specs=[rowd, rowd, pl.BlockSpec((d, d), lambda c: (0, 0)), vec, modc(2), vec, modc(3), modc(4), modc(5), vec,
                  col(d, 0), col(d, nc), col(CONV_W, 0), col(CONV_W, nc), col(1, 0), col(1, nc),
                  col(r, 0), col(r, nc), col(r, 2 * nc), col(r, 3 * nc),
                  pl.BlockSpec((cw, d), lambda c: (c, 0))],
        out_specs=(rowd, col(r, 0), col(r, 0)),
        scratch_shapes=[pltpu.VMEM((r, d), F32), pltpu.VMEM((r, d), BF16), pltpu.VMEM((r, d), F32)],
        compiler_params=_params("arbitrary"),
        name="attn_out_convffn_sample",
    )(x, o, wo_bf, g_mix_post, mod, g_ffn_pre, mod, mod, mod, g_ffn_post,
      wup_bf, wup_bf, conv_w, conv_w, conv_b, conv_b, hist, hist, hist, hist, wdn_bf)


def kernel(x_prompt, x_sample, cache_k_diff, cache_v_diff, cache_k_sb, cache_v_sb, state_conv, page_table,
           c_prompt, c_sample, w_ada, b_ada, g_mix_pre, g_mix_post, g_ffn_pre, g_ffn_post,
           w_qkv_diff, lambda_q1, lambda_k1, lambda_q2, lambda_k2, g_subln, w_o_diff,
           w_qkv_sb, w_o_sb, w_up, conv_w, conv_b, w_down):
    b, t_len, d = x_prompt.shape
    db = x_sample.shape[0]
    depth = w_ada.shape[0]
    ff = w_down.shape[1]
    nh_diff = d // LANES
    nh_sb = d // HEAD_LANES
    n_diff = w_qkv_diff.shape[0]
    n_sb = w_qkv_sb.shape[0]
    assert x_sample.shape[1] == 1 and t_len % ROW_TILE == 0 and ff % FF_CHUNK == 0

    mod = _modulation(jnp.concatenate([c_prompt, c_sample], axis=0), w_ada, b_ada)
    kfeat, qfeat = _alibi_tables(t_len, nh_diff)
    cache_kt_sb = jnp.transpose(cache_k_sb, (0, 1, 3, 4, 2))
    cache_vt_sb = jnp.transpose(cache_v_sb, (0, 1, 3, 4, 2))

    xp = x_prompt
    xs = x_sample.reshape(1, db, d)
    kd_p = vd_p = ks_p = vs_p = None
    outs = {n: [] for n in ("kd_s", "vd_s", "ks_s", "vs_s", "conv_p", "conv_s")}
    for layer in range(depth):
        j = layer // 2
        mod_p = mod[layer, :b].reshape(b, 1, 6 * d)
        mod_s = mod[layer, b:].reshape(1, db, 6 * d)
        g_pre = g_mix_pre[layer].reshape(1, d)
        is_diff = layer % 2 == 0
        w_qkv = (w_qkv_diff if is_diff else w_qkv_sb)[j]
        w_qkv_bf = w_qkv.astype(BF16)
        w_o = (w_o_diff if is_diff else w_o_sb)[j].astype(BF16)
        wup_bf = w_up[layer].astype(BF16)
        wdn_bf = w_down[layer].astype(BF16)

        qs, ksn, vsn = _qkv_sample(xs, mod_s, g_pre, w_qkv_bf)
        qs3, ksn3, vsn3 = (a.reshape(db, 1, d) for a in (qs, ksn, vsn))
        if is_diff:
            lam_init = 0.8 - 0.6 * math.exp(-0.3 * layer)
            lam_p = jnp.stack([lambda_q1[j], lambda_k1[j], lambda_q2[j], lambda_k2[j]])
            g_sub = g_subln[j].reshape(1, LANES)
            qp, kpb, vpb, kd_p, vd_p = _qkv_diff(xp, mod_p, g_pre, w_qkv_bf, kd_p, vd_p, j, n_diff, ROW_TILE)
            op = _diff_attention(qp, kpb, vpb, kfeat, qfeat, lam_p, g_sub, lam_init, ATTN_TILE_DIFF)
            os_ = _decode_diff(qs3, ksn3, vsn3, cache_k_diff, cache_v_diff, j, page_table, lam_p, g_sub, lam_init)
            outs["kd_s"].append(ksn.reshape(db, 1, nh_diff, LANES))
            outs["vd_s"].append(vsn.reshape(db, 1, nh_diff, LANES))
        else:
            wkt = w_qkv[:, d:2 * d].T.astype(BF16)
            wvt = w_qkv[:, 2 * d:3 * d].T.astype(BF16)
            qp, ktb, vtb, ks_p, vs_p = _qkv_sb(xp, mod_p, g_pre, w_qkv_bf[:, 0:d], wkt, wvt, ks_p, vs_p, j, n_sb,
                                               ROW_TILE)
            op = _sb_attention(qp, ktb, vtb, ATTN_TILE_SB)
            os_ = _decode_sb(qs3, cache_kt_sb, cache_vt_sb, j, page_table)
            outs["ks_s"].append(ksn.reshape(db, 1, nh_sb, HEAD_LANES))
            outs["vs_s"].append(vsn.reshape(db, 1, nh_sb, HEAD_LANES))

        gmp = g_mix_post[layer].reshape(1, d)
        gfp = g_ffn_pre[layer].reshape(1, d)
        gfo = g_ffn_post[layer].reshape(1, d)
        cb = conv_b[layer].reshape(1, 2 * ff)
        xp, hist_p = _post_ffn(xp, op, mod_p, w_o, gmp, gfp, wup_bf, conv_w[layer], cb, wdn_bf, gfo,
                               ROW_TILE, FF_CHUNK)
        hist_s = state_conv[layer].reshape(db, (CONV_W - 1) * 2 * ff)
        xs2, uv, ug = _sample_ffn(xs[0], os_.reshape(db, d), mod_s, w_o, gmp, gfp, wup_bf, conv_w[layer], cb,
                                  hist_s, wdn_bf, gfo, FF_CHUNK)
        xs = xs2.reshape(1, db, d)
        outs["conv_p"].append(hist_p)
        outs["conv_s"].append(jnp.stack([state_conv[layer][:, 1, :], jnp.concatenate([uv, ug], axis=1)], axis=1))

    st = {n: jnp.stack(v) for n, v in outs.items()}
    kd_p = kd_p.reshape(n_diff, b, t_len, nh_diff, LANES)
    vd_p = vd_p.reshape(n_diff, b, t_len, nh_diff, LANES)
    ks_p = jnp.transpose(ks_p.reshape(n_sb, b, nh_sb, HEAD_LANES, t_len), (0, 1, 4, 2, 3))
    vs_p = jnp.transpose(vs_p.reshape(n_sb, b, nh_sb, HEAD_LANES, t_len), (0, 1, 4, 2, 3))
    return (xp, xs.reshape(db, 1, d), kd_p, vd_p, ks_p, vs_p, st["conv_p"],
            st["kd_s"], st["vd_s"], st["ks_s"], st["vs_s"], st["conv_s"])
```
